```python
import math
import jax
import jax.numpy as jnp
from jax import lax
import numpy as np

D_MODEL = 4096
BATCH = 4
SEQ = 2048
DEPTH = 2
DEC_BATCH = 128
DEC_SEQ = 4
PAST_LEN = 16384
PAGE_SIZE = 128

H_A = 24
HD_A = 64
H_I = 32
D_I = 64
TOPK_MAX = 256
H_B = 8
D_B = 64
H_C = 12
D_NOPE = 128
D_ROPE = 32
D_VC = 128
Q_LORA = 768
R_KV = 128
ROPE_BASE = 10000.0
N_BUCKETS = 32
MAX_DIST = 128
PEER_HEADS = 8
N_KEYS = 128
N_EXPERTS = N_KEYS * N_KEYS
D_KEY = 128
PEER_TOPK = 16
PEER_BLOCK = 64
Q_BLOCK = 128
EPS = 1e-6

SPLITS = (H_A * HD_A, 2 * HD_A, H_I * D_I, H_I, D_I,
          H_B * 2 * D_B, 2 * D_B, 2 * D_B,
          Q_LORA, R_KV, D_ROPE,
          3 * D_MODEL)
D_IN = sum(SPLITS)
CACHE_A = 2 * HD_A + D_I
CACHE_B = 4 * D_B
CACHE_C = R_KV + D_ROPE

kernel_name = 'gated_hybrid_dsa_diff_mla_peer_step'


def rmsnorm(x, g):
    xf = x.astype(jnp.float32)
    y = xf * lax.rsqrt(jnp.mean(xf * xf, axis=-1, keepdims=True) + EPS)
    return (y * g.astype(jnp.float32)).astype(x.dtype)


def modulate(x, g, shift, scale):
    return rmsnorm(x, g) * (1.0 + scale[:, None]) + shift[:, None]


def rope(x, pos):
    half = x.shape[-1] // 2
    inv_freq = ROPE_BASE ** (-jnp.arange(half, dtype=jnp.float32) / half)
    ang = pos.astype(jnp.float32)[:, None, None] * inv_freq
    cos, sin = jnp.cos(ang), jnp.sin(ang)
    xf = x.astype(jnp.float32)
    x1, x2 = xf[..., :half], xf[..., half:]
    return jnp.concatenate([x1 * cos - x2 * sin, x1 * sin + x2 * cos], axis=-1).astype(x.dtype)


def rel_bucket(dist):
    n = jnp.maximum(dist, 0)
    max_exact = N_BUCKETS // 2
    nf = jnp.maximum(n, 1).astype(jnp.float32)
    large = max_exact + (jnp.log(nf / max_exact) / math.log(MAX_DIST / max_exact)
                         * (N_BUCKETS - max_exact)).astype(jnp.int32)
    return jnp.where(n < max_exact, n, jnp.minimum(large, N_BUCKETS - 1))


def project_tokens(h, pos, l, w_in, g_cq, w_uq, g_ckv, w_uk):
    N, T, _ = h.shape
    qa, kva, qi, wi, ki, qb, kb, vb, cq, ckv, kr, gt = jnp.split(
        h @ w_in[l], np.cumsum(SPLITS)[:-1].tolist(), axis=-1)
    q_c = (rmsnorm(cq, g_cq[l]) @ w_uq[l]).reshape(N, T, H_C, D_NOPE + D_ROPE)
    q_lat = jnp.einsum('nthd,rhd->nthr', q_c[..., :D_NOPE], w_uk[l])
    q_rope = rope(q_c[..., D_NOPE:], pos)
    k_rope = rope(kr[:, :, None, :], pos)[:, :, 0]
    return (qa.reshape(N, T, H_A, HD_A),
            qi.reshape(N, T, H_I, D_I),
            wi,
            jnp.concatenate([kva, ki], axis=-1),
            qb.reshape(N, T, H_B, 2, D_B),
            jnp.concatenate([kb, vb], axis=-1),
            q_lat,
            q_rope,
            jnp.concatenate([rmsnorm(ckv, g_ckv[l]), k_rope], axis=-1),
            gt.reshape(N, T, 3, D_MODEL))


def indexer_scores(qi, wi, kidx):
    dots = jnp.einsum('nthd,nsd->nths', qi.astype(jnp.float32), kidx.astype(jnp.float32))
    return jnp.einsum('nths,nth->nts', jax.nn.relu(dots), wi.astype(jnp.float32))


def dsa_select(scores, q_pos, k_pos, k_top):
    allowed = k_pos[None, :] <= q_pos[:, None]
    top_s, idx = lax.top_k(jnp.where(allowed[None], scores, -jnp.inf), k_top)
    return idx, top_s > -jnp.inf


def dsa_attend(q, k_sel, v_sel, idx, valid, q_pos, bias_tab):
    logits = jnp.einsum('nthd,ntkd->nthk', q, k_sel).astype(jnp.float32) * HD_A ** -0.5
    bias = bias_tab[rel_bucket(q_pos[None, :, None] - idx)]
    logits = logits + jnp.swapaxes(bias, -1, -2).astype(jnp.float32)
    p = jax.nn.softmax(jnp.where(valid[:, :, None, :], logits, -jnp.inf), axis=-1)
    return jnp.einsum('nthk,ntkd->nthd', p.astype(v_sel.dtype), v_sel)


def diff_attend(q1, q2, k1, k2, v, q_pos, k_pos, bias_tab, lam):
    allowed = k_pos[None, :] <= q_pos[:, None]
    bias = jnp.moveaxis(bias_tab[rel_bucket(q_pos[:, None] - k_pos[None, :])], -1, 0)
    bias = jnp.where(allowed[None], bias.astype(jnp.float32), -jnp.inf)[None]
    p1 = jax.nn.softmax(jnp.einsum('nthd,nsd->nhts', q1, k1).astype(jnp.float32) * D_B ** -0.5 + bias, axis=-1)
    p2 = jax.nn.softmax(jnp.einsum('nthd,nsd->nhts', q2, k2).astype(jnp.float32) * D_B ** -0.5 + bias, axis=-1)
    return jnp.einsum('nhts,nsd->nthd', (p1 - lam * p2).astype(v.dtype), v)


def mla_attend(q_lat, q_rope, ckv, krope, q_pos, k_pos):
    allowed = k_pos[None, :] <= q_pos[:, None]
    logits = (jnp.einsum('nthr,nsr->nhts', q_lat, ckv) + jnp.einsum('nthd,nsd->nhts', q_rope, krope)
              ).astype(jnp.float32) * (D_NOPE + D_ROPE) ** -0.5
    p = jax.nn.softmax(jnp.where(allowed, logits, -jnp.inf), axis=-1)
    return jnp.einsum('nhts,nsr->nthr', p.astype(ckv.dtype), ckv)


def unblock(o):
    o = jnp.swapaxes(o, 0, 1)
    return o.reshape(o.shape[0], -1, *o.shape[3:])


def prompt_attention(qa, qi, wi, ra, qb, rb, ql, qr, rc, bias_a, bias_b, lam):
    S = qa.shape[1]
    pos = jnp.arange(S)
    k_top = min(TOPK_MAX, S // 4)
    kv_a, kidx = ra[..., :2 * HD_A], ra[..., 2 * HD_A:]
    k1, k2, vb = rb[..., :D_B], rb[..., D_B:2 * D_B], rb[..., 2 * D_B:]
    ckv, krope = rc[..., :R_KV], rc[..., R_KV:]

    def block(start):
        sl = lambda a: lax.dynamic_slice_in_dim(a, start, Q_BLOCK, axis=1)
        q_pos = start + jnp.arange(Q_BLOCK)
        idx, valid = dsa_select(indexer_scores(sl(qi), sl(wi), kidx), q_pos, pos, k_top)
        sel = jax.vmap(lambda rows, i: rows[i])(kv_a, idx)
        y_a = dsa_attend(sl(qa), sel[..., :HD_A], sel[..., HD_A:], idx, valid, q_pos, bias_a)
        qb_blk = sl(qb)
        o_b = diff_attend(qb_blk[..., 0, :], qb_blk[..., 1, :], k1, k2, vb, q_pos, pos, bias_b, lam)
        lat = mla_attend(sl(ql), sl(qr), ckv, krope, q_pos, pos)
        return y_a, o_b, lat

    y_a, o_b, lat = lax.map(block, jnp.arange(S // Q_BLOCK) * Q_BLOCK)
    return unblock(y_a), unblock(o_b), unblock(lat)


def sample_attention(qa, qi, wi, ra, qb, rb, ql, qr, rc, cache_a, cache_b, cache_c, page_table,
                     l, bias_a, bias_b, lam):
    T = qa.shape[1]
    past = page_table.shape[1] * PAGE_SIZE
    q_pos = past + jnp.arange(T)
    k_pos = jnp.arange(past + T)
    k_top = min(TOPK_MAX, (past + T) // 4)

    def one_seq(args):
        pt, qa_s, qi_s, wi_s, ra_s, qb_s, rb_s, ql_s, qr_s, rc_s = args
        kidx = jnp.concatenate([cache_a[l, pt, :, 2 * HD_A:].reshape(past, D_I), ra_s[:, 2 * HD_A:]], axis=0)
        idx, valid = dsa_select(indexer_scores(qi_s[None], wi_s[None], kidx[None]), q_pos, k_pos, k_top)
        ic = jnp.minimum(idx, past - 1)
        sel = jnp.where((idx < past)[..., None],
                        cache_a[l, pt[ic // PAGE_SIZE], ic % PAGE_SIZE, :2 * HD_A],
                        ra_s[jnp.clip(idx - past, 0, T - 1), :2 * HD_A])
        y_a = dsa_attend(qa_s[None], sel[..., :HD_A], sel[..., HD_A:], idx, valid, q_pos, bias_a)
        kb = jnp.concatenate([cache_b[l, pt].reshape(past, CACHE_B), rb_s], axis=0)[None]
        o_b = diff_attend(qb_s[None, :, :, 0], qb_s[None, :, :, 1], kb[..., :D_B], kb[..., D_B:2 * D_B],
                          kb[..., 2 * D_B:], q_pos, k_pos, bias_b, lam)
        kc = jnp.concatenate([cache_c[l, pt].reshape(past, CACHE_C), rc_s], axis=0)[None]
        lat = mla_attend(ql_s[None], qr_s[None], kc[..., :R_KV], kc[..., R_KV:], q_pos, k_pos)
        return y_a[0], o_b[0], lat[0]

    return lax.map(one_seq, (page_table, qa, qi, wi, ra, qb, rb, ql, qr, rc))


def finish_mixer(y_a, o_b, lat, gates, l, lam_init, g_subln, w_uv, w_pa, w_pb, w_pc, w_o):
    N, T = y_a.shape[:2]
    y_b = rmsnorm(o_b, g_subln[l]) * (1.0 - lam_init)
    y_c = jnp.einsum('nthr,rhd->nthd', lat, w_uv[l])
    g = jax.nn.sigmoid(gates.astype(jnp.float32)).astype(y_a.dtype)
    merged = (g[:, :, 0] * (y_a.reshape(N, T, -1) @ w_pa[l])
              + g[:, :, 1] * (y_b.reshape(N, T, -1) @ w_pb[l])
              + g[:, :, 2] * (y_c.reshape(N, T, -1) @ w_pc[l]))
    return merged @ w_o[l]


def peer_ffn(h, l, peer_wq, peer_keys, peer_u, peer_v):
    N, T, D = h.shape
    n_tok = N * T
    x = h.reshape(n_tok, D)
    q = (x @ peer_wq[l]).reshape(n_tok, PEER_HEADS, 2, D_KEY // 2)
    s = jnp.einsum('nhpd,hpkd->nhpk', q, peer_keys[l]).astype(jnp.float32)
    s_top, i_top = lax.top_k(s, PEER_TOPK)
    cand = (s_top[:, :, 0, :, None] + s_top[:, :, 1, None, :]).reshape(n_tok, PEER_HEADS, PEER_TOPK * PEER_TOPK)
    cand_id = (i_top[:, :, 0, :, None] * N_KEYS + i_top[:, :, 1, None, :]).reshape(n_tok, PEER_HEADS, PEER_TOPK * PEER_TOPK)
    best, pick = lax.top_k(cand, PEER_TOPK)
    experts = jnp.take_along_axis(cand_id, pick, axis=-1)
    gate = jax.nn.softmax(best, axis=-1).astype(h.dtype)
    pad = (-n_tok) % PEER_BLOCK
    to_blocks = lambda a: jnp.pad(a, ((0, pad),) + ((0, 0),) * (a.ndim - 1)).reshape(-1, PEER_BLOCK, *a.shape[1:])

    def expert_block(args):
        xb, eb, gb = args
        act = jax.nn.gelu(jnp.einsum('nd,nhkd->nhk', xb, peer_u[l, eb]), approximate=False)
        return jnp.einsum('nhk,nhkd->nd', gb * act, peer_v[l, eb])

    out = lax.map(expert_block, (to_blocks(x), to_blocks(experts), to_blocks(gate)))
    return out.reshape(-1, D)[:n_tok].reshape(N, T, D)


def setup_inputs(seed: int = 0) -> dict:
    key = jax.random.key(seed)
    ks = iter(jax.random.split(key, 48))

    def nrm(shape, scale):
        return jax.random.normal(next(ks), shape, jnp.float32) * scale

    def gain(shape):
        return 1.0 + nrm(shape, 0.02)

    n_pages = PAST_LEN // PAGE_SIZE
    n_used = DEC_BATCH * n_pages
    n_pool = n_used + max(1, n_used // 4)
    page_table = jax.random.permutation(next(ks), n_pool)[:n_used].reshape(DEC_BATCH, n_pages).astype(jnp.int32)
    return {
        'x_prompt': nrm((BATCH, SEQ, D_MODEL), 1.0),
        'x_sample': nrm((DEC_BATCH, DEC_SEQ, D_MODEL), 1.0),
        'cache_a': nrm((DEPTH, n_pool, PAGE_SIZE, CACHE_A), 1.0),
        'cache_b': nrm((DEPTH, n_pool, PAGE_SIZE, CACHE_B), 1.0),
        'cache_c': nrm((DEPTH, n_pool, PAGE_SIZE, CACHE_C), 1.0),
        'page_table': page_table,
        'c_prompt': nrm((BATCH, D_MODEL), 1.0),
        'c_sample': nrm((DEC_BATCH, D_MODEL), 1.0),
        'w_ada': nrm((D_MODEL, 6 * D_MODEL), 0.5 * D_MODEL ** -0.5),
        'ada_emb': nrm((DEPTH, 6, D_MODEL), 0.02),
        'g_mix': gain((DEPTH, D_MODEL)),
        'w_in': nrm((DEPTH, D_MODEL, D_IN), D_MODEL ** -0.5),
        'g_cq': gain((DEPTH, Q_LORA)),
        'w_uq': nrm((DEPTH, Q_LORA, H_C * (D_NOPE + D_ROPE)), Q_LORA ** -0.5),
        'g_ckv': gain((DEPTH, R_KV)),
        'w_uk': nrm((DEPTH, R_KV, H_C, D_NOPE), D_NOPE ** -0.5),
        'w_uv': nrm((DEPTH, R_KV, H_C, D_VC), R_KV ** -0.5),
        'lam_q1': nrm((DEPTH, D_B), 0.1),
        'lam_k1': nrm((DEPTH, D_B), 0.1),
        'lam_q2': nrm((DEPTH, D_B), 0.1),
        'lam_k2': nrm((DEPTH, D_B), 0.1),
        'g_subln': gain((DEPTH, 2 * D_B)),
        'rel_bias': nrm((N_BUCKETS, H_A + H_B), 0.5),
        'w_pa': nrm((DEPTH, H_A * HD_A, D_MODEL), (H_A * HD_A) ** -0.5),
        'w_pb': nrm((DEPTH, H_B * 2 * D_B, D_MODEL), (H_B * 2 * D_B) ** -0.5),
        'w_pc': nrm((DEPTH, H_C * D_VC, D_MODEL), (H_C * D_VC) ** -0.5),
        'w_o': nrm((DEPTH, D_MODEL, D_MODEL), D_MODEL ** -0.5),
        'g_ffn': gain((DEPTH, D_MODEL)),
        'peer_wq': nrm((DEPTH, D_MODEL, PEER_HEADS * D_KEY), D_MODEL ** -0.5),
        'peer_keys': nrm((DEPTH, PEER_HEADS, 2, N_KEYS, D_KEY // 2), (D_KEY // 2) ** -0.5),
        'peer_u': nrm((DEPTH, N_EXPERTS, D_MODEL), D_MODEL ** -0.5),
        'peer_v': nrm((DEPTH, N_EXPERTS, D_MODEL), PEER_HEADS ** -0.5),
        'g_final': gain((D_MODEL,)),
    }


def reference(x_prompt, x_sample, cache_a, cache_b, cache_c, page_table, c_prompt, c_sample,
              w_ada, ada_emb, g_mix, w_in, g_cq, w_uq, g_ckv, w_uk, w_uv,
              lam_q1, lam_k1, lam_q2, lam_k2, g_subln, rel_bias, w_pa, w_pb, w_pc, w_o,
              g_ffn, peer_wq, peer_keys, peer_u, peer_v, g_final):
    mod_p = (jax.nn.silu(c_prompt) @ w_ada).reshape(c_prompt.shape[0], 6, D_MODEL)
    mod_s = (jax.nn.silu(c_sample) @ w_ada).reshape(c_sample.shape[0], 6, D_MODEL)
    bias_a, bias_b = rel_bias[:, :H_A], rel_bias[:, H_A:]
    pos_p = jnp.arange(x_prompt.shape[1])
    pos_s = page_table.shape[1] * PAGE_SIZE + jnp.arange(x_sample.shape[1])
    xp, xs = x_prompt, x_sample
    rows_a_p, rows_a_s, rows_b_p, rows_b_s, rows_c_p, rows_c_s = [], [], [], [], [], []
    for l in range(DEPTH):
        lam_init = 0.8 - 0.6 * math.exp(-0.3 * l)
        lam = (jnp.exp(jnp.sum(lam_q1[l].astype(jnp.float32) * lam_k1[l].astype(jnp.float32)))
               - jnp.exp(jnp.sum(lam_q2[l].astype(jnp.float32) * lam_k2[l].astype(jnp.float32))) + lam_init)
        mp = mod_p + ada_emb[l]
        ms = mod_s + ada_emb[l]
        qa, qi, wi, ra, qb, rb, ql, qr, rc, gt = project_tokens(
            modulate(xp, g_mix[l], mp[:, 0], mp[:, 1]), pos_p, l, w_in, g_cq, w_uq, g_ckv, w_uk)
        y_a, o_b, lat = prompt_attention(qa, qi, wi, ra, qb, rb, ql, qr, rc, bias_a, bias_b, lam)
        xp = xp + mp[:, None, 2] * finish_mixer(y_a, o_b, lat, gt, l, lam_init, g_subln, w_uv, w_pa, w_pb, w_pc, w_o)
        rows_a_p.append(ra)
        rows_b_p.append(rb)
        rows_c_p.append(rc)
        qa, qi, wi, ra, qb, rb, ql, qr, rc, gt = project_tokens(
            modulate(xs, g_mix[l], ms[:, 0], ms[:, 1]), pos_s, l, w_in, g_cq, w_uq, g_ckv, w_uk)
        y_a, o_b, lat = sample_attention(qa, qi, wi, ra, qb, rb, ql, qr, rc, cache_a, cache_b, cache_c,
                                         page_table, l, bias_a, bias_b, lam)
        xs = xs + ms[:, None, 2] * finish_mixer(y_a, o_b, lat, gt, l, lam_init, g_subln, w_uv, w_pa, w_pb, w_pc, w_o)
        rows_a_s.append(ra)
        rows_b_s.append(rb)
        rows_c_s.append(rc)
        xp = xp + mp[:, None, 5] * peer_ffn(modulate(xp, g_ffn[l], mp[:, 3], mp[:, 4]), l, peer_wq, peer_keys, peer_u, peer_v)
        xs = xs + ms[:, None, 5] * peer_ffn(modulate(xs, g_ffn[l], ms[:, 3], ms[:, 4]), l, peer_wq, peer_keys, peer_u, peer_v)
    y_prompt = rmsnorm(xp, g_final)
    y_sample = rmsnorm(xs, g_final)
    return (y_prompt, y_sample,
            jnp.stack(rows_a_p), jnp.stack(rows_a_s),
            jnp.stack(rows_b_p), jnp.stack(rows_b_s),
            jnp.stack(rows_c_p), jnp.stack(rows_c_s))
```

```python
import functools
import math

import numpy as np
import jax
import jax.numpy as jnp
from jax import lax
from jax.experimental import pallas as pl
from jax.experimental.pallas import tpu as pltpu

F32 = jnp.float32
BF16 = jnp.bfloat16
I32 = jnp.int32

H_A, HD_A, H_I, D_I, TOPK_MAX = 24, 64, 32, 64, 256
H_B, D_B = 8, 64
H_C, D_NOPE, D_ROPE, D_VC, Q_LORA, R_KV = 12, 128, 32, 128, 768, 128
ROPE_BASE = 10000.0
N_BUCKETS, MAX_DIST = 32, 128
PEER_HEADS, N_KEYS, D_KEY, PEER_TOPK = 8, 128, 128, 16
PAGE = 128
DEC_SEQ = 4
EPS = 1e-6
CACHE_A, CACHE_B, CACHE_C = 2 * HD_A + D_I, 4 * D_B, R_KV + D_ROPE

LANE = 128
NEG = -1e30
INT_MIN = np.int32(-2 ** 31)
LOG2E = 1.4426950408889634
TQ = 128
N_MAIN_BLOCKS = 54
VMEM_MB = 56


def _cparams(n_axes, vmem_mb=VMEM_MB):
    return pltpu.CompilerParams(dimension_semantics=("arbitrary",) * n_axes,
                                vmem_limit_bytes=vmem_mb * 1024 * 1024)


def _nt(a, b):
    return lax.dot_general(a, b, (((1,), (1,)), ((), ())), preferred_element_type=F32)


def _rms(x, g):
    return x * lax.rsqrt(jnp.mean(x * x, axis=-1, keepdims=True) + EPS) * g


def _mm_body(x_ref, w_ref, o_ref, *, act, pre):
    x = x_ref[...]
    if pre == "silu":
        x = (x * jax.nn.sigmoid(x)).astype(BF16)
    acc = jnp.dot(x, w_ref[...].astype(BF16), preferred_element_type=F32)
    if act == "sigmoid":
        acc = jax.nn.sigmoid(acc)
    o_ref[...] = acc.astype(o_ref.dtype)


def _mm(x, w, tm, tn, out_dtype, act=None, pre=None):
    m, kd = x.shape
    n = w.shape[1]
    return pl.pallas_call(
        functools.partial(_mm_body, act=act, pre=pre),
        grid=(n // tn, m // tm),
        in_specs=[pl.BlockSpec((tm, kd), lambda j, i: (i, 0)),
                  pl.BlockSpec((kd, tn), lambda j, i: (0, j))],
        out_specs=pl.BlockSpec((tm, tn), lambda j, i: (i, j)),
        out_shape=jax.ShapeDtypeStruct((m, n), out_dtype),
        compiler_params=_cparams(2),
    )(x, w)


class _Group:
    def __init__(self, m, mods, rows_per_mod):
        self.m = m
        self.mods = mods
        self.rows_per_mod = rows_per_mod

    def mod_spec(self, tm, tn, ij):
        if self.rows_per_mod is None:
            return pl.BlockSpec((tm, tn), lambda *g: ij(*g))
        r = self.rows_per_mod
        return pl.BlockSpec((None, 1, tn), lambda *g: ((ij(*g)[0] * tm) // r, 0, ij(*g)[1]))


def _modulate_body(x_ref, g_ref, sc_ref, sh_ref, asc_ref, ash_ref, o_ref):
    y = _rms(x_ref[...], g_ref[...])
    o_ref[...] = (y * (1.0 + sc_ref[...] + asc_ref[...]) + sh_ref[...] + ash_ref[...]).astype(BF16)


def _modulate(x, g, grp, ks, kb, ada, tm):
    m, d = x.shape
    ij = lambda i: (i, 0)
    row = pl.BlockSpec((1, d), lambda i: (0, 0))
    return pl.pallas_call(
        _modulate_body,
        grid=(m // tm,),
        in_specs=[pl.BlockSpec((tm, d), lambda i: (i, 0)), row,
                  grp.mod_spec(tm, d, ij), grp.mod_spec(tm, d, ij), row, row],
        out_specs=pl.BlockSpec((tm, d), lambda i: (i, 0)),
        out_shape=jax.ShapeDtypeStruct((m, d), BF16),
        compiler_params=_cparams(1),
    )(x, g.reshape(1, d), grp.mods[ks], grp.mods[kb], ada[ks:ks + 1], ada[kb:kb + 1])


def _resid_body(x_ref, y_ref, gt_ref, ada_ref, o_ref):
    o_ref[...] = x_ref[...] + (gt_ref[...] + ada_ref[...]) * y_ref[...]


def _resid(x, y, grp, kg, ada, tm):
    m, d = x.shape
    ij = lambda i: (i, 0)
    blk = pl.BlockSpec((tm, d), lambda i: (i, 0))
    return pl.pallas_call(
        _resid_body,
        grid=(m // tm,),
        in_specs=[blk, blk, grp.mod_spec(tm, d, ij), pl.BlockSpec((1, d), lambda i: (0, 0))],
        out_specs=blk,
        out_shape=jax.ShapeDtypeStruct((m, d), F32),
        compiler_params=_cparams(1),
    )(x, y, grp.mods[kg], ada[kg:kg + 1])


def _final_norm_body(x_ref, g_ref, o_ref):
    o_ref[...] = _rms(x_ref[...], g_ref[...])


def _final_norm(x, g, tm):
    m, d = x.shape
    blk = pl.BlockSpec((tm, d), lambda i: (i, 0))
    return pl.pallas_call(
        _final_norm_body, grid=(m // tm,),
        in_specs=[blk, pl.BlockSpec((1, d), lambda i: (0, 0))], out_specs=blk,
        out_shape=jax.ShapeDtypeStruct((m, d), F32), compiler_params=_cparams(1),
    )(x, g.reshape(1, d))


B_QA, B_KE, B_KO, B_VE, B_VO, B_KI1, B_KI2, B_QI, B_QB = 0, 12, 13, 14, 15, 16, 17, 18, 34
B_K1, B_K2, B_VB, B_CQ, B_CKV, B_KR, B_KRS = 42, 43, 44, 45, 51, 52, 53


def _prep_body(p_ref, gcq_ref, gckv_ref, wuq_ref, wuk_ref, cos_ref, sin_ref,
               qa_ref, ke_ref, ko_ref, ve_ref, vo_ref, kie_ref, kio_ref, wi_ref, qi_ref, qb_ref,
               k1_ref, k2_ref, vb_ref, qc_ref, rck_ref, ra_ref, rb_ref, rc_ref):
    blk = lambda b, n=1: p_ref[:, b * LANE:(b + n) * LANE]
    sa = HD_A ** -0.5 * LOG2E
    for p in range(H_A // 2):
        qa_ref[p] = (blk(B_QA + p) * sa).astype(BF16)
    ke, ko, ve, vo = blk(B_KE), blk(B_KO), blk(B_VE), blk(B_VO)
    ke_ref[...] = ke.astype(BF16)
    ko_ref[...] = ko.astype(BF16)
    ve_ref[...] = ve.astype(BF16)
    vo_ref[...] = vo.astype(BF16)
    ki1, ki2 = blk(B_KI1), blk(B_KI2)
    lane = lax.broadcasted_iota(I32, ki2.shape, 1)
    kie_ref[...] = ki1.astype(BF16)
    kio_ref[...] = jnp.where(lane >= D_I, ki2, 0.0).astype(BF16)
    wi_ref[...] = ki2
    for p in range(H_I // 2):
        qi_ref[p] = blk(B_QI + p).astype(BF16)
    sb = D_B ** -0.5 * LOG2E
    for p in range(H_B):
        qb_ref[p] = (blk(B_QB + p) * sb).astype(BF16)
    k1, k2, vb = blk(B_K1), blk(B_K2), blk(B_VB)
    k1_ref[...] = k1.astype(BF16)
    k2_ref[...] = k2.astype(BF16)
    vb_ref[...] = vb.astype(BF16)
    ra_ref[:, 0:LANE] = ke + vo
    ra_ref[:, LANE:LANE + D_I] = ki1[:, 0:D_I]
    rb_ref[:, 0:LANE] = k1 + k2
    rb_ref[:, LANE:2 * LANE] = vb
    cqn = _rms(blk(B_CQ, Q_LORA // LANE), gcq_ref[...]).astype(BF16)
    qc = jnp.dot(cqn, wuq_ref[...], preferred_element_type=F32)
    cos, sin = cos_ref[...], sin_ref[...]
    sc = (D_NOPE + D_ROPE) ** -0.5 * LOG2E
    nb = H_C
    for h in range(H_C):
        nope = qc[:, h * LANE:(h + 1) * LANE].astype(BF16)
        qlat = jnp.dot(nope, wuk_ref[h], preferred_element_type=F32)
        rp = qc[:, (nb + h) * LANE:(nb + h + 1) * LANE] * cos + qc[:, (2 * nb + h) * LANE:(2 * nb + h + 1) * LANE] * sin
        qc_ref[h, :, 0:LANE] = (qlat * sc).astype(BF16)
        qc_ref[h, :, LANE:2 * LANE] = (rp * sc).astype(BF16)
    ckvn = _rms(blk(B_CKV), gckv_ref[...])
    krope = blk(B_KR) * cos + blk(B_KRS) * sin
    rck_ref[:, 0:LANE] = ckvn.astype(BF16)
    rck_ref[:, LANE:2 * LANE] = krope.astype(BF16)
    rc_ref[:, 0:LANE] = ckvn
    rc_ref[:, LANE:LANE + D_ROPE] = krope[:, 0:D_ROPE]


def _prep(p, gcq, gckv, wuq, wuk, cos_t, sin_t, pos_tiles, tm):
    m = p.shape[0]
    tok = lambda w, dt: (jax.ShapeDtypeStruct((m, w), dt), pl.BlockSpec((tm, w), lambda i: (i, 0)))
    hm = lambda h, w: (jax.ShapeDtypeStruct((h, m, w), BF16), pl.BlockSpec((h, tm, w), lambda i: (0, i, 0)))
    outs = [hm(H_A // 2, LANE)] + [tok(LANE, BF16)] * 6 + [tok(LANE, F32), hm(H_I // 2, LANE), hm(H_B, LANE)] \
        + [tok(LANE, BF16)] * 3 + [hm(H_C, 2 * LANE), tok(2 * LANE, BF16),
                                   tok(CACHE_A, F32), tok(CACHE_B, F32), tok(CACHE_C, F32)]
    full = lambda a: pl.BlockSpec(a.shape, lambda i: (0,) * a.ndim)
    tab = pl.BlockSpec((tm, LANE), lambda i: (i % pos_tiles, 0))
    return pl.pallas_call(
        _prep_body, grid=(m // tm,),
        in_specs=[pl.BlockSpec((tm, p.shape[1]), lambda i: (i, 0)), full(gcq), full(gckv), full(wuq), full(wuk), tab, tab],
        out_specs=[o[1] for o in outs], out_shape=[o[0] for o in outs],
        compiler_params=_cparams(1),
    )(p, gcq, gckv, wuq, wuk, cos_t, sin_t)


def _flash_init(m_ref, l_ref, acc_ref):
    m_ref[...] = jnp.full(m_ref.shape, NEG, F32)
    l_ref[...] = jnp.zeros(l_ref.shape, F32)
    acc_ref[...] = jnp.zeros(acc_ref.shape, F32)


def _flash_update(g, s, v, m_ref, l_ref, acc_ref):
    m_old = m_ref[g]
    m_new = jnp.maximum(m_old, jnp.max(s, axis=1, keepdims=True))
    alpha = jnp.exp2(m_old - m_new)
    p = jnp.exp2(s - m_new)
    l_ref[g] = alpha * l_ref[g] + jnp.sum(p, axis=1, keepdims=True)
    acc_ref[g] = alpha * acc_ref[g] + jnp.dot(p.astype(BF16), v, preferred_element_type=F32)
    m_ref[g] = m_new


def _prompt_sweep(i, q_all, k_refs, v_of, add_far, add_near, m_ref, l_ref, acc_ref):
    n_far = jnp.maximum(i - 1, 0)

    def chunk(start, ck, add):
        for g, k_ref in enumerate(k_refs):
            s = add(g, _nt(q_all, k_ref[pl.ds(start, ck), :]))
            _flash_update(g, s, v_of(g, start, ck), m_ref, l_ref, acc_ref)

    def far_body(jj, carry):
        chunk(pl.multiple_of(jj * 2 * TQ, 2 * TQ), 2 * TQ, functools.partial(add_far, blk=2 * jj, nb=2))
        return carry

    lax.fori_loop(0, n_far // 2, far_body, 0)

    @pl.when(n_far % 2 == 1)
    def _():
        chunk(pl.multiple_of((n_far - 1) * TQ, TQ), TQ, functools.partial(add_far, blk=n_far - 1, nb=1))

    chunk(pl.multiple_of(n_far * TQ, TQ), 2 * TQ, add_near)


def _score_keys(score, allowed=None):
    bits = lax.bitcast_convert_type(score, I32)
    key = bits ^ ((bits >> 31) & np.int32(0x7FFFFFFF))
    return key if allowed is None else jnp.where(allowed, key, INT_MIN)


def _kth_largest(count_ge, shape, k):
    def bit_body(b, t):
        cand = t ^ jnp.left_shift(np.int32(1), 31 - b)
        return jnp.where(count_ge(cand) >= k, cand, t)
    return lax.fori_loop(0, 32, bit_body, jnp.full(shape, INT_MIN, I32))


def _tri(n):
    r = lax.broadcasted_iota(I32, (n, n), 0)
    c = lax.broadcasted_iota(I32, (n, n), 1)
    return jnp.where(r <= c, 1.0, 0.0).astype(BF16)


def _select_block(key, thr, need, carry, tri):
    eq = (key == thr) & (thr != INT_MIN)
    eqf = jnp.where(eq, 1.0, 0.0)
    pref = jnp.dot(eqf.astype(BF16), tri, preferred_element_type=F32) + carry
    sel = (key > thr) | (eq & (pref <= need))
    return jnp.where(sel, 0.0, NEG), carry + jnp.sum(eqf, axis=1, keepdims=True)


def _prompt_a_body(qa_ref, qi_ref, wi_ref, ke_ref, ko_ref, ve_ref, vo_ref, kie_ref, kio_ref, bias_ref, o_ref,
                   key_ref, mask_ref, m_ref, l_ref, acc_ref, *, k_top):
    i = pl.program_id(1)
    n_act = i + 1
    np_i = H_I // 2
    qi_all = qi_ref[...].reshape(np_i * TQ, LANE)
    wi = wi_ref[...]
    row = lax.broadcasted_iota(I32, (TQ, LANE), 0)
    col = lax.broadcasted_iota(I32, (TQ, LANE), 1)

    def idx_body(j, carry):
        st = pl.multiple_of(j * LANE, LANE)
        se = _nt(qi_all, kie_ref[pl.ds(st, LANE), :]).reshape(np_i, TQ, LANE)
        so = _nt(qi_all, kio_ref[pl.ds(st, LANE), :]).reshape(np_i, TQ, LANE)
        tot = jnp.zeros((TQ, LANE), F32)
        for p in range(np_i):
            tot = tot + jnp.maximum(se[p], 0.0) * wi[:, 2 * p:2 * p + 1] + jnp.maximum(so[p], 0.0) * wi[:, 2 * p + 1:2 * p + 2]
        key_ref[j] = _score_keys(tot, (j * LANE + col) <= (i * TQ + row))
        return carry

    lax.fori_loop(0, n_act, idx_body, 0)

    def count(pred_of):
        def body(j, c):
            return c + jnp.sum(jnp.where(pred_of(key_ref[j]), 1.0, 0.0), axis=1, keepdims=True)
        return lax.fori_loop(0, n_act, body, jnp.zeros((TQ, 1), F32))

    thr = _kth_largest(lambda cand: count(lambda k: k >= cand), (TQ, 1), float(k_top))
    need = float(k_top) - count(lambda k: k > thr)
    tri = _tri(LANE)

    def mask_body(j, carry):
        mask_ref[j], carry = _select_block(key_ref[j], thr, need, carry, tri)
        return carry

    lax.fori_loop(0, n_act, mask_body, jnp.zeros((TQ, 1), F32))

    @pl.when(i == 0)
    def _():
        mask_ref[1] = jnp.full((TQ, LANE), NEG, F32)

    np_a = H_A // 2
    _flash_init(m_ref, l_ref, acc_ref)
    q_all = qa_ref[...].reshape(np_a * TQ, LANE)
    v_refs = (ve_ref, vo_ref)
    var = jnp.where(i == 0, 1, 0)

    def mask_cat(blk, nb):
        return mask_ref[blk] if nb == 1 else jnp.concatenate([mask_ref[blk], mask_ref[blk + 1]], axis=1)

    def add_far(g, s, blk, nb):
        return (s.reshape(np_a, TQ, nb * TQ) + mask_cat(blk, nb)[None]).reshape(np_a * TQ, nb * TQ)

    def add_near(g, s):
        mk = mask_cat(jnp.maximum(i - 1, 0), 2)
        return (s.reshape(np_a, TQ, 2 * TQ) + bias_ref[var, g] + mk[None]).reshape(np_a * TQ, 2 * TQ)

    _prompt_sweep(i, q_all, (ke_ref, ko_ref), lambda g, st, ck: v_refs[g][pl.ds(st, ck), :],
                  add_far, add_near, m_ref, l_ref, acc_ref)
    out = acc_ref[0] / l_ref[0] + acc_ref[1] / l_ref[1]
    for p in range(np_a):
        o_ref[:, p * LANE:(p + 1) * LANE] = out[p * TQ:(p + 1) * TQ].astype(BF16)


def _prompt_a(qa, qi, wi, ke, ko, ve, vo, kie, kio, bias, n, s):
    kv = pl.BlockSpec((None, s, LANE), lambda b, i: (b, 0, 0))
    hm = lambda h: pl.BlockSpec((h, None, TQ, LANE), lambda b, i: (0, b, i, 0))
    r = (H_A // 2) * TQ
    nblk = max(s // LANE, 2)
    return pl.pallas_call(
        functools.partial(_prompt_a_body, k_top=min(TOPK_MAX, s // 4)),
        grid=(n, s // TQ),
        in_specs=[hm(H_A // 2), hm(H_I // 2), pl.BlockSpec((None, TQ, LANE), lambda b, i: (b, i, 0)),
                  kv, kv, kv, kv, kv, kv, pl.BlockSpec(bias.shape, lambda b, i: (0,) * bias.ndim)],
        out_specs=pl.BlockSpec((None, TQ, H_A * HD_A), lambda b, i: (b, i, 0)),
        out_shape=jax.ShapeDtypeStruct((n, s, H_A * HD_A), BF16),
        scratch_shapes=[pltpu.VMEM((nblk, TQ, LANE), I32), pltpu.VMEM((nblk, TQ, LANE), F32),
                        pltpu.VMEM((2, r, 1), F32), pltpu.VMEM((2, r, 1), F32), pltpu.VMEM((2, r, LANE), F32)],
        compiler_params=_cparams(2),
    )(qa, qi, wi, ke, ko, ve, vo, kie, kio, bias)


def _lambda_of(lamv, lam_init):
    return (jnp.exp(jnp.sum(lamv[0:1] * lamv[1:2], axis=1, keepdims=True))
            - jnp.exp(jnp.sum(lamv[2:3] * lamv[3:4], axis=1, keepdims=True)) + lam_init)


def _prompt_b_body(qb_ref, k1_ref, k2_ref, vb_ref, bias_ref, lam_ref, g_ref, o_ref, m_ref, l_ref, acc_ref, *, lam_init):
    i = pl.program_id(1)
    _flash_init(m_ref, l_ref, acc_ref)
    q_all = qb_ref[...].reshape(H_B * TQ, LANE)
    var = jnp.where(i == 0, 1, 0)

    def add_near(g, s):
        return (s.reshape(H_B, TQ, 2 * TQ) + bias_ref[var]).reshape(H_B * TQ, 2 * TQ)

    _prompt_sweep(i, q_all, (k1_ref, k2_ref), lambda g, st, ck: vb_ref[pl.ds(st, ck), :],
                  lambda g, s, blk, nb: s, add_near, m_ref, l_ref, acc_ref)
    lam = _lambda_of(lam_ref[...], lam_init)
    o = acc_ref[0] / l_ref[0] - lam * (acc_ref[1] / l_ref[1])
    y = _rms(o, g_ref[...]) * (1.0 - lam_init)
    for h in range(H_B):
        o_ref[:, h * LANE:(h + 1) * LANE] = y[h * TQ:(h + 1) * TQ].astype(BF16)


def _prompt_b(qb, k1, k2, vb, bias, lamv, g, lam_init, n, s):
    kv = pl.BlockSpec((None, s, LANE), lambda b, i: (b, 0, 0))
    full = lambda a: pl.BlockSpec(a.shape, lambda b, i: (0,) * a.ndim)
    r = H_B * TQ
    return pl.pallas_call(
        functools.partial(_prompt_b_body, lam_init=lam_init),
        grid=(n, s // TQ),
        in_specs=[pl.BlockSpec((H_B, None, TQ, LANE), lambda b, i: (0, b, i, 0)), kv, kv, kv, full(bias), full(lamv), full(g)],
        out_specs=pl.BlockSpec((None, TQ, H_B * 2 * D_B), lambda b, i: (b, i, 0)),
        out_shape=jax.ShapeDtypeStruct((n, s, H_B * 2 * D_B), BF16),
        scratch_shapes=[pltpu.VMEM((2, r, 1), F32), pltpu.VMEM((2, r, 1), F32), pltpu.VMEM((2, r, LANE), F32)],
        compiler_params=_cparams(2),
    )(qb, k1, k2, vb, bias, lamv, g)


def _prompt_c_body(qc_ref, rck_ref, cm_ref, wuv_ref, o_ref, m_ref, l_ref, acc_ref):
    i = pl.program_id(1)
    _flash_init(m_ref, l_ref, acc_ref)
    q_all = qc_ref[...].reshape(H_C * TQ, 2 * LANE)
    var = jnp.where(i == 0, 1, 0)

    def add_near(g, s):
        return (s.reshape(H_C, TQ, 2 * TQ) + cm_ref[var][None]).reshape(H_C * TQ, 2 * TQ)

    _prompt_sweep(i, q_all, (rck_ref,), lambda g, st, ck: rck_ref[pl.ds(st, ck), 0:LANE],
                  lambda g, s, blk, nb: s, add_near, m_ref, l_ref, acc_ref)
    lat = (acc_ref[0] / l_ref[0]).astype(BF16)
    for h in range(H_C):
        y = jnp.dot(lat[h * TQ:(h + 1) * TQ], wuv_ref[h], preferred_element_type=F32)
        o_ref[:, h * LANE:(h + 1) * LANE] = y.astype(BF16)


def _prompt_c(qc, rck, cmask, wuv, n, s):
    full = lambda a: pl.BlockSpec(a.shape, lambda b, i: (0,) * a.ndim)
    r = H_C * TQ
    return pl.pallas_call(
        _prompt_c_body,
        grid=(n, s // TQ),
        in_specs=[pl.BlockSpec((H_C, None, TQ, 2 * LANE), lambda b, i: (0, b, i, 0)),
                  pl.BlockSpec((None, s, 2 * LANE), lambda b, i: (b, 0, 0)), full(cmask), full(wuv)],
        out_specs=pl.BlockSpec((None, TQ, H_C * D_VC), lambda b, i: (b, i, 0)),
        out_shape=jax.ShapeDtypeStruct((n, s, H_C * D_VC), BF16),
        scratch_shapes=[pltpu.VMEM((1, r, 1), F32), pltpu.VMEM((1, r, 1), F32), pltpu.VMEM((1, r, LANE), F32)],
        compiler_params=_cparams(2),
    )(qc, rck, cmask, wuv)


def _page_specs(pps, n_pages, width, layer):
    def spec(j):
        return pl.BlockSpec((None, None, PAGE, width),
                            lambda b, c, pt: (layer, pt[b * n_pages + c * pps + j], 0, 0))
    return [spec(j) for j in range(pps)]


def _sample_select_body(pt_ref, *refs, pps, k_top):
    pages = refs[:pps]
    new_ref, qi_ref, w_ref, mask_ref, masknew_ref, kbuf_ref, key_ref, keynew_ref = refs[pps:]
    c = pl.program_id(1)
    n_ch = pl.num_programs(1)
    ck = pps * PAGE
    rows = DEC_SEQ * H_I

    def scores(kidx):
        s = jnp.maximum(_nt(qi_ref[...], kidx), 0.0) * w_ref[...]
        return jnp.sum(s.reshape(DEC_SEQ, H_I, kidx.shape[0]), axis=1)

    for j in range(pps):
        kbuf_ref[j * PAGE:(j + 1) * PAGE, :] = pages[j][:, 2 * HD_A:CACHE_A].astype(BF16)
    sc = scores(kbuf_ref[...])
    key_ref[c] = _score_keys(sc)

    @pl.when(c == n_ch - 1)
    def _():
        sn = scores(new_ref[:, 2 * HD_A:CACHE_A].astype(BF16))
        t = lax.broadcasted_iota(I32, (DEC_SEQ, PAGE), 0)
        jk = lax.broadcasted_iota(I32, (DEC_SEQ, PAGE), 1)
        keynew_ref[...] = _score_keys(sn, jk <= t)

        def count(pred_of):
            def body(j, acc):
                return acc + jnp.sum(jnp.where(pred_of(key_ref[j]), 1.0, 0.0), axis=1, keepdims=True)
            init = jnp.sum(jnp.where(pred_of(keynew_ref[...]), 1.0, 0.0), axis=1, keepdims=True)
            return lax.fori_loop(0, n_ch, body, init)

        kf = float(k_top)
        thr = _kth_largest(lambda cand: count(lambda k: k >= cand), (DEC_SEQ, 1), kf)
        need = kf - count(lambda k: k > thr)
        has_tie = (thr != INT_MIN) & (count(lambda k: k >= thr) > kf)

        def plain(key):
            return jnp.where((key >= thr) & (key != INT_MIN), 0.0, NEG)

        def fill(j, carry):
            mask_ref[j] = plain(key_ref[j])
            return carry

        lax.fori_loop(0, n_ch, fill, 0)
        masknew_ref[...] = plain(keynew_ref[...])

        @pl.when(jnp.max(jnp.where(has_tie, 1.0, 0.0)) > 0.0)
        def _():
            tri = _tri(PAGE)

            def chunk_body(j, carry):
                for u in range(pps):
                    mk, carry = _select_block(key_ref[j, :, u * PAGE:(u + 1) * PAGE], thr, need, carry, tri)
                    mask_ref[j, :, u * PAGE:(u + 1) * PAGE] = mk
                return carry

            carry = lax.fori_loop(0, n_ch, chunk_body, jnp.zeros((DEC_SEQ, 1), F32))
            masknew_ref[...], _ = _select_block(keynew_ref[...], thr, need, carry, tri)


def _sample_select(pt_flat, cache_a, layer, new_a, qi_rows, w_col, n, n_pages, pps):
    n_ch = n_pages // pps
    ck = pps * PAGE
    per = lambda shp: pl.BlockSpec((None,) + shp, lambda b, c, pt: (b,) + (0,) * len(shp))
    grid_spec = pltpu.PrefetchScalarGridSpec(
        num_scalar_prefetch=1, grid=(n, n_ch),
        in_specs=_page_specs(pps, n_pages, CACHE_A, layer)
        + [per((PAGE, CACHE_A)), per((DEC_SEQ * H_I, D_I)), per((DEC_SEQ * H_I, 1))],
        out_specs=[per((n_ch, DEC_SEQ, ck)), per((DEC_SEQ, PAGE))],
        scratch_shapes=[pltpu.VMEM((ck, D_I), BF16), pltpu.VMEM((n_ch, DEC_SEQ, ck), I32), pltpu.VMEM((DEC_SEQ, PAGE), I32)])
    return pl.pallas_call(
        functools.partial(_sample_select_body, pps=pps, k_top=min(TOPK_MAX, (n_pages * PAGE + DEC_SEQ) // 4)),
        grid_spec=grid_spec,
        out_shape=[jax.ShapeDtypeStruct((n, n_ch, DEC_SEQ, ck), F32), jax.ShapeDtypeStruct((n, DEC_SEQ, PAGE), F32)],
        compiler_params=_cparams(2),
    )(pt_flat, *([cache_a] * pps), new_a, qi_rows, w_col)


RA_ROWS, RB_ROWS, RC_ROWS = DEC_SEQ * H_A, 2 * DEC_SEQ * H_B, DEC_SEQ * H_C


def _sample_attend_body(pt_ref, *refs, pps, lam_init):
    pa, pb, pc = refs[:pps], refs[pps:2 * pps], refs[2 * pps:3 * pps]
    (newa_ref, newb_ref, newc_ref, qa_ref, qb_ref, qc_ref, mask_ref, masknew_ref,
     ba_last_ref, ba_new_ref, bb_last_ref, bb_new_ref, cc_new_ref, lam_ref, g_ref,
     oa_ref, ob_ref, oc_ref,
     kba_ref, kbb_ref, kbc_ref, ma_ref, la_ref, acca_ref, mb_ref, lb_ref, accb_ref, mc_ref, lc_ref, accc_ref) = refs[3 * pps:]
    c = pl.program_id(1)
    n_ch = pl.num_programs(1)
    ck = pps * PAGE
    last = jnp.where(c == n_ch - 1, 1.0, 0.0)

    @pl.when(c == 0)
    def _():
        _flash_init(ma_ref, la_ref, acca_ref)
        _flash_init(mb_ref, lb_ref, accb_ref)
        _flash_init(mc_ref, lc_ref, accc_ref)
        kbc_ref[...] = jnp.zeros(kbc_ref.shape, BF16)

    for j in range(pps):
        rows = slice(j * PAGE, (j + 1) * PAGE)
        kba_ref[rows, :] = pa[j][:, 0:LANE].astype(BF16)
        kbb_ref[rows, :] = pb[j][...].astype(BF16)
        kbc_ref[rows, 0:CACHE_C] = pc[j][...].astype(BF16)

    def band(s, bias_ref):
        return jnp.concatenate([s[:, :ck - PAGE], s[:, ck - PAGE:] + last * bias_ref[...]], axis=1)

    def sel_rows(mask):
        return mask[:, None, :]

    ka = kba_ref[...]
    sa = band(_nt(qa_ref[...], ka), ba_last_ref)
    sa = (sa.reshape(DEC_SEQ, H_A, ck) + sel_rows(mask_ref[...])).reshape(RA_ROWS, ck)
    _flash_update(0, sa, ka, ma_ref, la_ref, acca_ref)
    sb = band(_nt(qb_ref[...], kbb_ref[:, 0:LANE]), bb_last_ref)
    _flash_update(0, sb, kbb_ref[:, LANE:2 * LANE], mb_ref, lb_ref, accb_ref)
    kc = kbc_ref[...]
    _flash_update(0, _nt(qc_ref[...], kc), kc[:, 0:LANE], mc_ref, lc_ref, accc_ref)

    @pl.when(c == n_ch - 1)
    def _():
        na = newa_ref[:, 0:LANE].astype(BF16)
        sna = _nt(qa_ref[...], na) + ba_new_ref[...]
        sna = (sna.reshape(DEC_SEQ, H_A, PAGE) + sel_rows(masknew_ref[...])).reshape(RA_ROWS, PAGE)
        _flash_update(0, sna, na, ma_ref, la_ref, acca_ref)
        nb = newb_ref[...].astype(BF16)
        _flash_update(0, _nt(qb_ref[...], nb[:, 0:LANE]) + bb_new_ref[...], nb[:, LANE:2 * LANE], mb_ref, lb_ref, accb_ref)
        ncz = jnp.concatenate([newc_ref[...], jnp.zeros((PAGE, 2 * LANE - CACHE_C), F32)], axis=1).astype(BF16)
        _flash_update(0, _nt(qc_ref[...], ncz) + cc_new_ref[...], ncz[:, 0:LANE], mc_ref, lc_ref, accc_ref)
        oa_ref[...] = acca_ref[0] / la_ref[0]
        ob = accb_ref[0] / lb_ref[0]
        half = RB_ROWS // 2
        o = ob[0:half] - _lambda_of(lam_ref[...], lam_init) * ob[half:RB_ROWS]
        ob_ref[...] = (_rms(o, g_ref[...]) * (1.0 - lam_init)).astype(BF16)
        oc_ref[...] = (accc_ref[0] / lc_ref[0]).astype(BF16)


def _sample_attend(pt_flat, cache_a, cache_b, cache_c, layer, new_a, new_b, new_c, qa_rows, qb_rows, qc_rows,
                   mask, mask_new, tiles, lamv, g, lam_init, n, n_pages, pps):
    n_ch = n_pages // pps
    ck = pps * PAGE
    per = lambda shp: pl.BlockSpec((None,) + shp, lambda b, c, pt: (b,) + (0,) * len(shp))
    full = lambda a: pl.BlockSpec(a.shape, lambda b, c, pt: (0,) * a.ndim)
    flash = lambda r: [pltpu.VMEM((1, r, 1), F32), pltpu.VMEM((1, r, 1), F32), pltpu.VMEM((1, r, LANE), F32)]
    grid_spec = pltpu.PrefetchScalarGridSpec(
        num_scalar_prefetch=1, grid=(n, n_ch),
        in_specs=_page_specs(pps, n_pages, CACHE_A, layer) + _page_specs(pps, n_pages, CACHE_B, layer)
        + _page_specs(pps, n_pages, CACHE_C, layer)
        + [per((PAGE, CACHE_A)), per((PAGE, CACHE_B)), per((PAGE, CACHE_C)),
           per((RA_ROWS, LANE)), per((RB_ROWS, LANE)), per((RC_ROWS, 2 * LANE)),
           pl.BlockSpec((None, None, DEC_SEQ, ck), lambda b, c, pt: (b, c, 0, 0)), per((DEC_SEQ, PAGE))]
        + [full(t) for t in tiles] + [full(lamv), full(g)],
        out_specs=[per((RA_ROWS, LANE)), per((RB_ROWS // 2, LANE)), per((RC_ROWS, LANE))],
        scratch_shapes=[pltpu.VMEM((ck, LANE), BF16), pltpu.VMEM((ck, 2 * LANE), BF16), pltpu.VMEM((ck, 2 * LANE), BF16)]
        + flash(RA_ROWS) + flash(RB_ROWS) + flash(RC_ROWS))
    return pl.pallas_call(
        functools.partial(_sample_attend_body, pps=pps, lam_init=lam_init),
        grid_spec=grid_spec,
        out_shape=[jax.ShapeDtypeStruct((n, RA_ROWS, LANE), F32), jax.ShapeDtypeStruct((n, RB_ROWS // 2, LANE), BF16),
                   jax.ShapeDtypeStruct((n, RC_ROWS, LANE), BF16)],
        compiler_params=_cparams(2),
    )(pt_flat, *([cache_a] * pps), *([cache_b] * pps), *([cache_c] * pps), new_a, new_b, new_c,
      qa_rows, qb_rows, qc_rows, mask, mask_new, *tiles, lamv, g)


def _uv_body(lat_ref, w_ref, o_ref):
    o_ref[...] = jnp.dot(lat_ref[...], w_ref[...], preferred_element_type=F32).astype(BF16)


def _uv(lat, wuv):
    h, m, _ = lat.shape
    return pl.pallas_call(
        _uv_body, grid=(h,),
        in_specs=[pl.BlockSpec((None, m, LANE), lambda i: (i, 0, 0)), pl.BlockSpec((None, LANE, LANE), lambda i: (i, 0, 0))],
        out_specs=pl.BlockSpec((None, m, LANE), lambda i: (i, 0, 0)),
        out_shape=jax.ShapeDtypeStruct((h, m, LANE), BF16), compiler_params=_cparams(1),
    )(lat, wuv)


def _merge_body(ya_ref, yb_ref, yc_ref, wa_ref, wb_ref, wc_ref, g0_ref, g1_ref, g2_ref, o_ref):
    dot = lambda a, b: jnp.dot(a[...], b[...], preferred_element_type=F32)
    o_ref[...] = (g0_ref[...].astype(F32) * dot(ya_ref, wa_ref) + g1_ref[...].astype(F32) * dot(yb_ref, wb_ref)
                  + g2_ref[...].astype(F32) * dot(yc_ref, wc_ref)).astype(BF16)


def _merge(ya, yb, yc, wa, wb, wc, gates, tm, tn):
    m = ya.shape[0]
    d = wa.shape[1]
    nt = d // tn
    xs = lambda a: pl.BlockSpec((tm, a.shape[1]), lambda j, i: (i, 0))
    ws = lambda a: pl.BlockSpec((a.shape[0], tn), lambda j, i: (0, j))
    gs = lambda k: pl.BlockSpec((tm, tn), lambda j, i: (i, k * nt + j))
    return pl.pallas_call(
        _merge_body, grid=(nt, m // tm),
        in_specs=[xs(ya), xs(yb), xs(yc), ws(wa), ws(wb), ws(wc), gs(0), gs(1), gs(2)],
        out_specs=pl.BlockSpec((tm, tn), lambda j, i: (i, j)),
        out_shape=jax.ShapeDtypeStruct((m, d), BF16), compiler_params=_cparams(2),
    )(ya, yb, yc, wa, wb, wc, gates, gates, gates)


def _proj_resid_body(a_ref, w_ref, x_ref, gt_ref, ada_ref, o_ref):
    y = jnp.dot(a_ref[...], w_ref[...], preferred_element_type=F32)
    o_ref[...] = x_ref[...] + (gt_ref[...] + ada_ref[...]) * y


def _proj_resid(a, w, x, grp, kg, ada, tm, tn):
    m, kd = a.shape
    d = w.shape[1]
    return pl.pallas_call(
        _proj_resid_body, grid=(d // tn, m // tm),
        in_specs=[pl.BlockSpec((tm, kd), lambda j, i: (i, 0)), pl.BlockSpec((kd, tn), lambda j, i: (0, j)),
                  pl.BlockSpec((tm, tn), lambda j, i: (i, j)), grp.mod_spec(tm, tn, lambda j, i: (i, j)),
                  pl.BlockSpec((1, tn), lambda j, i: (0, j))],
        out_specs=pl.BlockSpec((tm, tn), lambda j, i: (i, j)),
        out_shape=jax.ShapeDtypeStruct((m, d), F32), compiler_params=_cparams(2),
    )(a, w, x, grp.mods[kg], ada[kg:kg + 1])


def _top_rows(x, k):
    vals = []
    for _ in range(k):
        m = jnp.max(x, axis=0, keepdims=True)
        vals.append(m)
        x = jnp.where(x == m, -jnp.inf, x)
    return jnp.concatenate(vals, axis=0)


def _peer_select_body(q_ref, k1_ref, k2_ref, s1_ref, e1_ref, s2_ref, e2_ref, thr_ref):
    thr_rows = []
    for h in range(PEER_HEADS):
        qp = q_ref[:, h * D_KEY:(h + 1) * D_KEY]
        s1 = _nt(k1_ref[h], qp)
        s2 = _nt(k2_ref[h], qp)
        a1 = _top_rows(s1, PEER_TOPK)
        a2 = _top_rows(s2, PEER_TOPK)
        cand = jnp.concatenate([a1[r:r + 1] + a2 for r in range(PEER_TOPK)], axis=0)
        best = _top_rows(cand, PEER_TOPK)
        z = jnp.sum(jnp.exp(best - best[0:1]), axis=0, keepdims=True)
        s1_ref[h] = s1
        s2_ref[h] = s2
        e1_ref[h] = jnp.exp(s1 - a1[0:1])
        e2_ref[h] = (jnp.exp(s2 - a2[0:1]) / z).astype(BF16)
        thr_rows.append(best[PEER_TOPK - 1:PEER_TOPK])
    thr_ref[...] = jnp.concatenate(thr_rows, axis=0)


def _peer_select(q, k1, k2, tm):
    m = q.shape[0]
    hk = lambda dt: (jax.ShapeDtypeStruct((PEER_HEADS, N_KEYS, m), dt),
                     pl.BlockSpec((PEER_HEADS, N_KEYS, tm), lambda i: (0, 0, i)))
    outs = [hk(F32), hk(F32), hk(F32), hk(BF16),
            (jax.ShapeDtypeStruct((PEER_HEADS, m), F32), pl.BlockSpec((PEER_HEADS, tm), lambda i: (0, i)))]
    full = lambda a: pl.BlockSpec(a.shape, lambda i: (0,) * a.ndim)
    return pl.pallas_call(
        _peer_select_body, grid=(m // tm,),
        in_specs=[pl.BlockSpec((tm, q.shape[1]), lambda i: (i, 0)), full(k1), full(k2)],
        out_specs=[o[1] for o in outs], out_shape=[o[0] for o in outs],
        compiler_params=_cparams(1),
    )(q, k1, k2)


def _peer_body(h_ref, u_ref, v_ref, s1_ref, e1_ref, s2_ref, e2_ref, thr_ref, o_ref, *, eb):
    e = pl.program_id(1)

    @pl.when(e == 0)
    def _():
        o_ref[...] = jnp.zeros(o_ref.shape, F32)

    a_t = _nt(u_ref[...], h_ref[...])
    act = 0.5 * a_t * (1.0 + lax.erf(a_t * (2.0 ** -0.5)))
    rows = []
    for il in range(eb // N_KEYS):
        g = jnp.zeros((N_KEYS, a_t.shape[1]), F32)
        for h in range(PEER_HEADS):
            ssum = s1_ref[il, h:h + 1, :] + s2_ref[h]
            gate = e1_ref[il, h:h + 1, :] * e2_ref[h].astype(F32)
            g = g + jnp.where(ssum >= thr_ref[h:h + 1, :], gate, 0.0)
        rows.append(g * act[il * N_KEYS:(il + 1) * N_KEYS])
    ga = jnp.concatenate(rows, axis=0).T.astype(BF16)
    o_ref[...] += jnp.dot(ga, v_ref[...], preferred_element_type=F32)


def _peer(h, u, v, s1, e1, s2, e2, thr, tm, eb):
    m, d = h.shape
    ne = u.shape[0]
    sel = lambda: pl.BlockSpec((PEER_HEADS, N_KEYS, tm), lambda i, e: (0, 0, i))
    rows = lambda: pl.BlockSpec((eb // N_KEYS, PEER_HEADS, tm), lambda i, e: (e, 0, i))
    return pl.pallas_call(
        functools.partial(_peer_body, eb=eb), grid=(m // tm, ne // eb),
        in_specs=[pl.BlockSpec((tm, d), lambda i, e: (i, 0)), pl.BlockSpec((eb, d), lambda i, e: (e, 0)),
                  pl.BlockSpec((eb, d), lambda i, e: (e, 0)), rows(), rows(), sel(), sel(),
                  pl.BlockSpec((PEER_HEADS, tm), lambda i, e: (0, i))],
        out_specs=pl.BlockSpec((tm, d), lambda i, e: (i, 0)),
        out_shape=jax.ShapeDtypeStruct((m, d), F32), compiler_params=_cparams(2),
    )(h, u, v, s1, e1, s2, e2, thr)


def _rel_bucket(dist):
    n = jnp.maximum(dist, 0)
    max_exact = N_BUCKETS // 2
    nf = jnp.maximum(n, 1).astype(F32)
    large = max_exact + (jnp.log(nf / max_exact) / math.log(MAX_DIST / max_exact) * (N_BUCKETS - max_exact)).astype(I32)
    return jnp.where(n < max_exact, n, jnp.minimum(large, N_BUCKETS - 1))


def _bias_of_dist(rel_bias, dist):
    tab = (rel_bias - rel_bias[N_BUCKETS - 1:N_BUCKETS]) * LOG2E
    return jnp.where((dist >= 0)[..., None], tab[_rel_bucket(dist)], NEG)


def _prompt_bias_tiles(rel_bias):
    t = jnp.arange(TQ)[:, None]
    s = jnp.arange(2 * TQ)[None, :]
    d_mid = t + TQ - s
    d_first = jnp.where(s < TQ, t - s, -1)
    tiles = jnp.stack([_bias_of_dist(rel_bias, d_mid), _bias_of_dist(rel_bias, d_first)])
    tiles = jnp.moveaxis(tiles, -1, 1)
    ta = tiles[:, :H_A].reshape(2, H_A // 2, 2, TQ, 2 * TQ).swapaxes(1, 2)
    tb = tiles[:, H_A:]
    cm = jnp.stack([jnp.where(d_mid >= 0, 0.0, NEG), jnp.where(d_first >= 0, 0.0, NEG)]).astype(F32)
    return ta, tb, cm


def _sample_bias_tiles(rel_bias, past):
    t = jnp.arange(DEC_SEQ)
    s_last = past - PAGE + jnp.arange(PAGE)
    d_last = past + t[:, None] - s_last[None, :]
    j = jnp.arange(PAGE)
    d_new = jnp.where(j[None, :] < DEC_SEQ, t[:, None] - j[None, :], -1)
    b_last = _bias_of_dist(rel_bias, d_last)
    b_new = _bias_of_dist(rel_bias, d_new)
    th = lambda b, h0, h1: jnp.moveaxis(b[..., h0:h1], -1, 1).reshape(-1, PAGE)
    ba_last, ba_new = th(b_last, 0, H_A), th(b_new, 0, H_A)
    bb_last = jnp.tile(th(b_last, H_A, H_A + H_B), (2, 1))
    bb_new = jnp.tile(th(b_new, H_A, H_A + H_B), (2, 1))
    cc_new = jnp.repeat(jnp.where(d_new >= 0, 0.0, NEG).astype(F32), H_C, axis=0)
    return [ba_last, ba_new, bb_last, bb_new, cc_new]


def _rope_tables(pos):
    half = D_ROPE // 2
    inv_freq = ROPE_BASE ** (-jnp.arange(half, dtype=F32) / half)
    ang = pos.astype(F32)[:, None] * inv_freq
    cos, sin = jnp.cos(ang), jnp.sin(ang)
    pad = jnp.zeros((pos.shape[0], LANE - D_ROPE), F32)
    return jnp.concatenate([cos, cos, pad], axis=1), jnp.concatenate([-sin, sin, pad], axis=1)


def _swap_halves(w):
    h = w.shape[-1] // 2
    return jnp.concatenate([w[..., h:], w[..., :h]], axis=-1)


def _layer_weights(l, d, w_in, w_uq, w_uk, w_uv):
    splits = (H_A * HD_A, 2 * HD_A, H_I * D_I, H_I, D_I, H_B * 2 * D_B, 2 * D_B, 2 * D_B, Q_LORA, R_KV, D_ROPE, 3 * d)
    cs = np.cumsum((0,) + splits)
    qa, kva, qi, wi, ki, qb, kb, vb, cq, ckv, kr, gt = [w_in[l][:, cs[k]:cs[k + 1]] for k in range(12)]
    z = lambda n: jnp.zeros((d, n), F32)
    k_a, v_a, k1, k2 = kva[:, :HD_A], kva[:, HD_A:], kb[:, :D_B], kb[:, D_B:]
    main = jnp.concatenate([
        qa, k_a, z(64), z(64), k_a, v_a, z(64), z(64), v_a, ki, z(64), wi, z(32), ki, qi, qb,
        k1, z(64), z(64), k2, vb, cq, ckv, kr, z(LANE - D_ROPE), _swap_halves(kr), z(LANE - D_ROPE)], axis=1)
    assert main.shape[1] == N_MAIN_BLOCKS * LANE
    uq = w_uq[l].reshape(Q_LORA, H_C, D_NOPE + D_ROPE)
    rope_w = uq[:, :, D_NOPE:]
    padr = lambda w: jnp.concatenate([w, jnp.zeros((Q_LORA, H_C, LANE - D_ROPE), F32)], axis=-1).reshape(Q_LORA, H_C * LANE)
    uq_ext = jnp.concatenate([uq[:, :, :D_NOPE].reshape(Q_LORA, H_C * D_NOPE), padr(rope_w), padr(_swap_halves(rope_w))], axis=1)
    return (main.astype(BF16), gt.astype(BF16), uq_ext.astype(BF16),
            jnp.transpose(w_uk[l], (1, 2, 0)).astype(BF16), jnp.transpose(w_uv[l], (1, 0, 2)).astype(BF16))


def _tile(m, cap):
    t = min(m, cap)
    assert m % t == 0
    return t


def _project(x, grp, l, wts, g_mix, ada, g_cq, g_ckv, cos_t, sin_t, pos_tiles, tm_prep):
    w_main, w_gate, uq_ext, ukT, _ = wts
    m = x.shape[0]
    h = _modulate(x, g_mix[l], grp, 1, 0, ada, _tile(m, 256))
    tm = _tile(m, 1024)
    p = _mm(h, w_main, tm, 768, F32)
    gates = _mm(h, w_gate, tm, 512, BF16, act="sigmoid")
    outs = _prep(p, g_cq[l].reshape(1, -1), g_ckv[l].reshape(1, -1), uq_ext, ukT, cos_t, sin_t, pos_tiles, tm_prep)
    return outs, gates


def _peer_ffn(x, grp, l, ada, g_ffn, wq, k1p, k2p, u, v):
    m, d = x.shape
    h = _modulate(x, g_ffn[l], grp, 4, 3, ada, _tile(m, 256))
    q = _mm(h, wq, _tile(m, 1024), 512, BF16)
    tm = _tile(m, 512)
    s1, e1, s2, e2, thr = _peer_select(q, k1p, k2p, _tile(m, 256))
    y = _peer(h, u, v, jnp.swapaxes(s1, 0, 1), jnp.swapaxes(e1, 0, 1), s2, e2, thr, tm, 512)
    return _resid(x, y, grp, 5, ada, _tile(m, 256))


def kernel(x_prompt, x_sample, cache_a, cache_b, cache_c, page_table, c_prompt, c_sample, w_ada, ada_emb, g_mix, w_in, g_cq, w_uq, g_ckv, w_uk, w_uv, lam_q1, lam_k1, lam_q2, lam_k2, g_subln, rel_bias, w_pa, w_pb, w_pc, w_o, g_ffn, peer_wq, peer_keys, peer_u, peer_v, g_final):
    nb, seq, d = x_prompt.shape
    ns, ts, _ = x_sample.shape
    assert ts == DEC_SEQ and seq % (2 * TQ) == 0
    depth = w_in.shape[0]
    n_pages = page_table.shape[1]
    past = n_pages * PAGE
    pps = min(16, n_pages)
    mp, ms = nb * seq, ns * ts

    c_all = jnp.concatenate([c_prompt, c_sample], axis=0)
    mod = _mm(c_all, w_ada, nb + ns, 512, F32, pre="silu").reshape(nb + ns, 6, d)
    grp_p = _Group(mp, [mod[:nb, k].reshape(nb, 1, d) for k in range(6)], seq)
    grp_s = _Group(ms, [jnp.repeat(mod[nb:, k], ts, axis=0) for k in range(6)], None)

    cos_p, sin_p = _rope_tables(jnp.arange(seq))
    cos_s, sin_s = _rope_tables(jnp.tile(past + jnp.arange(ts), ns))
    bias_a, bias_b, cmask = _prompt_bias_tiles(rel_bias)
    s_tiles = _sample_bias_tiles(rel_bias, past)
    pt_flat = page_table.reshape(-1)

    xp = x_prompt.reshape(mp, d)
    xs = x_sample.reshape(ms, d)
    rows = [[] for _ in range(6)]
    tmp_prep = _tile(seq, 256)
    tms_prep = _tile(ms, 256)
    for l in range(depth):
        lam_init = 0.8 - 0.6 * math.exp(-0.3 * l)
        lamv = jnp.stack([lam_q1[l], lam_k1[l], lam_q2[l], lam_k2[l]]).astype(F32)
        gsub = g_subln[l].reshape(1, 2 * D_B)
        ada = ada_emb[l]
        wts = _layer_weights(l, d, w_in, w_uq, w_uk, w_uv)
        wuv = wts[4]
        wpa, wpb, wpc, wo = (w.astype(BF16) for w in (w_pa[l], w_pb[l], w_pc[l], w_o[l]))

        (qa, ke, ko, ve, vo, kie, kio, wi, qi, qb, k1, k2, vb, qc, rck, ra, rb, rc), gates = _project(
            xp, grp_p, l, wts, g_mix, ada, g_cq, g_ckv, cos_p, sin_p, seq // tmp_prep, tmp_prep)
        s3 = lambda a: a.reshape(nb, seq, a.shape[-1])
        h4 = lambda a: a.reshape(a.shape[0], nb, seq, a.shape[-1])
        ya = _prompt_a(h4(qa), h4(qi), s3(wi), s3(ke), s3(ko), s3(ve), s3(vo), s3(kie), s3(kio), bias_a, nb, seq)
        yb = _prompt_b(h4(qb), s3(k1), s3(k2), s3(vb), bias_b, lamv, gsub, lam_init, nb, seq)
        yc = _prompt_c(h4(qc), s3(rck), cmask, wuv, nb, seq)
        tm = _tile(seq, 1024)
        merged = _merge(ya.reshape(mp, -1), yb.reshape(mp, -1), yc.reshape(mp, -1), wpa, wpb, wpc, gates, tm, 512)
        xp = _proj_resid(merged, wo, xp, grp_p, 2, ada, tm, 512)
        for k, r in zip((0, 2, 4), (ra, rb, rc)):
            rows[k].append(r.reshape(nb, seq, -1))

        (qa, ke, ko, ve, vo, kie, kio, wi, qi, qb, k1, k2, vb, qc, rck, ra, rb, rc), gates = _project(
            xs, grp_s, l, wts, g_mix, ada, g_cq, g_ckv, cos_s, sin_s, ms // tms_prep, tms_prep)
        pad_page = lambda r: jnp.pad(r.reshape(ns, ts, -1), ((0, 0), (0, PAGE - ts), (0, 0)))
        new_a, new_b, new_c = pad_page(ra), pad_page(rb), pad_page(rc)
        heads = lambda a, w: jnp.moveaxis(a, 0, 1).reshape(ns, ts, -1, w)
        qi_rows = heads(qi, D_I).reshape(ns, ts * H_I, D_I)
        w_col = wi.reshape(ns, ts, LANE)[:, :, :H_I].reshape(ns, ts * H_I, 1)
        mask, mask_new = _sample_select(pt_flat, cache_a, l, new_a, qi_rows, w_col, ns, n_pages, pps)
        qa_rows = jnp.pad(heads(qa, HD_A).reshape(ns, RA_ROWS, HD_A), ((0, 0), (0, 0), (0, LANE - HD_A)))
        qb4 = heads(qb, D_B).reshape(ns, ts, H_B, 2, D_B)
        zq = jnp.zeros((ns, ts, H_B, D_B), BF16)
        qb_rows = jnp.concatenate([jnp.concatenate([qb4[..., 0, :], zq], axis=-1).reshape(ns, ts * H_B, LANE),
                                   jnp.concatenate([zq, qb4[..., 1, :]], axis=-1).reshape(ns, ts * H_B, LANE)], axis=1)
        qc_rows = jnp.moveaxis(qc, 0, 1).reshape(ns, RC_ROWS, 2 * LANE)
        oa, ob, oc = _sample_attend(pt_flat, cache_a, cache_b, cache_c, l, new_a, new_b, new_c, qa_rows, qb_rows, qc_rows,
                                    mask, mask_new, s_tiles, lamv, gsub, lam_init, ns, n_pages, pps)
        ya = oa[:, :, HD_A:].astype(BF16).reshape(ms, H_A * HD_A)
        yb = ob.reshape(ms, H_B * 2 * D_B)
        lat = jnp.moveaxis(oc.reshape(ms, H_C, LANE), 1, 0)
        yc = jnp.moveaxis(_uv(lat, wuv), 0, 1).reshape(ms, H_C * D_VC)
        merged = _merge(ya, yb, yc, wpa, wpb, wpc, gates, ms, 512)
        xs = _proj_resid(merged, wo, xs, grp_s, 2, ada, ms, 512)
        for k, r in zip((1, 3, 5), (ra, rb, rc)):
            rows[k].append(r.reshape(ns, ts, -1))

        wq = peer_wq[l].astype(BF16)
        zk = jnp.zeros((PEER_HEADS, N_KEYS, D_KEY // 2), F32)
        k1p = jnp.concatenate([peer_keys[l, :, 0], zk], axis=-1).astype(BF16)
        k2p = jnp.concatenate([zk, peer_keys[l, :, 1]], axis=-1).astype(BF16)
        u, v = peer_u[l].astype(BF16), peer_v[l].astype(BF16)
        xp = _peer_ffn(xp, grp_p, l, ada, g_ffn, wq, k1p, k2p, u, v)
        xs = _peer_ffn(xs, grp_s, l, ada, g_ffn, wq, k1p, k2p, u, v)

    y_prompt = _final_norm(xp, g_final, _tile(mp, 256)).reshape(nb, seq, d)
    y_sample = _final_norm(xs, g_final, _tile(ms, 256)).reshape(ns, ts, d)
    return (y_prompt, y_sample) + tuple(jnp.stack(r) for r in rows)
```

```python
import functools
import math

import numpy as np
import jax
import jax.numpy as jnp
from jax import lax
from jax.experimental import pallas as pl
from jax.experimental.pallas import tpu as pltpu

F32 = jnp.float32
BF16 = jnp.bfloat16
I32 = jnp.int32

H_A, HD_A, H_I, D_I, TOPK_MAX = 24, 64, 32, 64, 256
H_B, D_B = 8, 64
H_C, D_NOPE, D_ROPE, D_VC, Q_LORA, R_KV = 12, 128, 32, 128, 768, 128
ROPE_BASE = 10000.0
N_BUCKETS, MAX_DIST = 32, 128
PEER_HEADS, N_KEYS, D_KEY, PEER_TOPK = 8, 128, 128, 16
PAGE = 128
DEC_SEQ = 4
EPS = 1e-6
CACHE_A, CACHE_B, CACHE_C = 2 * HD_A + D_I, 4 * D_B, R_KV + D_ROPE

LANE = 128
NEG = -1e30
INT_MIN = np.int32(-2 ** 31)
LOG2E = 1.4426950408889634
TQ = 128
N_MAIN_BLOCKS = 54
VMEM_MB = 56


def _cparams(n_axes, vmem_mb=VMEM_MB):
    return pltpu.CompilerParams(dimension_semantics=("arbitrary",) * n_axes,
                                vmem_limit_bytes=vmem_mb * 1024 * 1024)


def _nt(a, b):
    return lax.dot_general(a, b, (((1,), (1,)), ((), ())), preferred_element_type=F32)


def _rms(x, g):
    return x * lax.rsqrt(jnp.mean(x * x, axis=-1, keepdims=True) + EPS) * g


def _mm_body(x_ref, w_ref, o_ref, *, act, pre):
    x = x_ref[...]
    if pre == "silu":
        x = (x * jax.nn.sigmoid(x)).astype(BF16)
    acc = jnp.dot(x, w_ref[...].astype(BF16), preferred_element_type=F32)
    if act == "sigmoid":
        acc = jax.nn.sigmoid(acc)
    o_ref[...] = acc.astype(o_ref.dtype)


def _mm(x, w, tm, tn, out_dtype, act=None, pre=None):
    m, kd = x.shape
    n = w.shape[1]
    return pl.pallas_call(
        functools.partial(_mm_body, act=act, pre=pre),
        grid=(n // tn, m // tm),
        in_specs=[pl.BlockSpec((tm, kd), lambda j, i: (i, 0)),
                  pl.BlockSpec((kd, tn), lambda j, i: (0, j))],
        out_specs=pl.BlockSpec((tm, tn), lambda j, i: (i, j)),
        out_shape=jax.ShapeDtypeStruct((m, n), out_dtype),
        compiler_params=_cparams(2), name="mm_%s" % (act or pre or "plain"),
    )(x, w)


class _Group:
    def __init__(self, m, mods, rows_per_mod):
        self.m = m
        self.mods = mods
        self.rows_per_mod = rows_per_mod

    def mod_spec(self, tm, tn, ij):
        if self.rows_per_mod is None:
            return pl.BlockSpec((tm, tn), lambda *g: ij(*g))
        r = self.rows_per_mod
        return pl.BlockSpec((None, 1, tn), lambda *g: ((ij(*g)[0] * tm) // r, 0, ij(*g)[1]))


def _modulate_body(x_ref, g_ref, sc_ref, sh_ref, asc_ref, ash_ref, o_ref):
    y = _rms(x_ref[...], g_ref[...])
    o_ref[...] = (y * (1.0 + sc_ref[...] + asc_ref[...]) + sh_ref[...] + ash_ref[...]).astype(BF16)


def _modulate(x, g, grp, ks, kb, ada, tm):
    m, d = x.shape
    ij = lambda i: (i, 0)
    row = pl.BlockSpec((1, d), lambda i: (0, 0))
    return pl.pallas_call(
        _modulate_body,
        grid=(m // tm,),
        in_specs=[pl.BlockSpec((tm, d), lambda i: (i, 0)), row,
                  grp.mod_spec(tm, d, ij), grp.mod_spec(tm, d, ij), row, row],
        out_specs=pl.BlockSpec((tm, d), lambda i: (i, 0)),
        out_shape=jax.ShapeDtypeStruct((m, d), BF16),
        compiler_params=_cparams(1),
    )(x, g.reshape(1, d), grp.mods[ks], grp.mods[kb], ada[ks:ks + 1], ada[kb:kb + 1])


def _resid_body(x_ref, y_ref, gt_ref, ada_ref, o_ref):
    o_ref[...] = x_ref[...] + (gt_ref[...] + ada_ref[...]) * y_ref[...]


def _resid(x, y, grp, kg, ada, tm):
    m, d = x.shape
    ij = lambda i: (i, 0)
    blk = pl.BlockSpec((tm, d), lambda i: (i, 0))
    return pl.pallas_call(
        _resid_body,
        grid=(m // tm,),
        in_specs=[blk, blk, grp.mod_spec(tm, d, ij), pl.BlockSpec((1, d), lambda i: (0, 0))],
        out_specs=blk,
        out_shape=jax.ShapeDtypeStruct((m, d), F32),
        compiler_params=_cparams(1),
    )(x, y, grp.mods[kg], ada[kg:kg + 1])


def _final_norm_body(x_ref, g_ref, o_ref):
    o_ref[...] = _rms(x_ref[...], g_ref[...])


def _final_norm(x, g, tm):
    m, d = x.shape
    blk = pl.BlockSpec((tm, d), lambda i: (i, 0))
    return pl.pallas_call(
        _final_norm_body, grid=(m // tm,),
        in_specs=[blk, pl.BlockSpec((1, d), lambda i: (0, 0))], out_specs=blk,
        out_shape=jax.ShapeDtypeStruct((m, d), F32), compiler_params=_cparams(1),
    )(x, g.reshape(1, d))


B_QA, B_KE, B_KO, B_VE, B_VO, B_KI1, B_KI2, B_QI, B_QB = 0, 12, 13, 14, 15, 16, 17, 18, 34
B_K1, B_K2, B_VB, B_CQ, B_CKV, B_KR, B_KRS = 42, 43, 44, 45, 51, 52, 53


def _prep_body(p_ref, gcq_ref, gckv_ref, wuq_ref, wuk_ref, cos_ref, sin_ref,
               qa_ref, ke_ref, ko_ref, ve_ref, vo_ref, kie_ref, kio_ref, wi_ref, qi_ref, qb_ref,
               k1_ref, k2_ref, vb_ref, qc_ref, rck_ref, ra_ref, rb_ref, rc_ref):
    blk = lambda b, n=1: p_ref[:, b * LANE:(b + n) * LANE]
    sa = HD_A ** -0.5 * LOG2E
    for p in range(H_A // 2):
        qa_ref[p] = (blk(B_QA + p) * sa).astype(BF16)
    ke, ko, ve, vo = blk(B_KE), blk(B_KO), blk(B_VE), blk(B_VO)
    ke_ref[...] = ke.astype(BF16)
    ko_ref[...] = ko.astype(BF16)
    ve_ref[...] = ve.astype(BF16)
    vo_ref[...] = vo.astype(BF16)
    ki1, ki2 = blk(B_KI1), blk(B_KI2)
    lane = lax.broadcasted_iota(I32, ki2.shape, 1)
    kie_ref[...] = ki1.astype(BF16)
    kio_ref[...] = jnp.where(lane >= D_I, ki2, 0.0).astype(BF16)
    wi_ref[...] = ki2
    for p in range(H_I // 2):
        qi_ref[p] = blk(B_QI + p).astype(BF16)
    sb = D_B ** -0.5 * LOG2E
    for p in range(H_B):
        qb_ref[p] = (blk(B_QB + p) * sb).astype(BF16)
    k1, k2, vb = blk(B_K1), blk(B_K2), blk(B_VB)
    k1_ref[...] = k1.astype(BF16)
    k2_ref[...] = k2.astype(BF16)
    vb_ref[...] = vb.astype(BF16)
    ra_ref[:, 0:LANE] = ke + vo
    ra_ref[:, LANE:LANE + D_I] = ki1[:, 0:D_I]
    rb_ref[:, 0:LANE] = k1 + k2
    rb_ref[:, LANE:2 * LANE] = vb
    cqn = _rms(blk(B_CQ, Q_LORA // LANE), gcq_ref[...]).astype(BF16)
    qc = jnp.dot(cqn, wuq_ref[...], preferred_element_type=F32)
    cos, sin = cos_ref[...], sin_ref[...]
    sc = (D_NOPE + D_ROPE) ** -0.5 * LOG2E
    nb = H_C
    for h in range(H_C):
        nope = qc[:, h * LANE:(h + 1) * LANE].astype(BF16)
        qlat = jnp.dot(nope, wuk_ref[h], preferred_element_type=F32)
        rp = qc[:, (nb + h) * LANE:(nb + h + 1) * LANE] * cos + qc[:, (2 * nb + h) * LANE:(2 * nb + h + 1) * LANE] * sin
        qc_ref[h, :, 0:LANE] = (qlat * sc).astype(BF16)
        qc_ref[h, :, LANE:2 * LANE] = (rp * sc).astype(BF16)
    ckvn = _rms(blk(B_CKV), gckv_ref[...])
    krope = blk(B_KR) * cos + blk(B_KRS) * sin
    rck_ref[:, 0:LANE] = ckvn.astype(BF16)
    rck_ref[:, LANE:2 * LANE] = krope.astype(BF16)
    rc_ref[:, 0:LANE] = ckvn
    rc_ref[:, LANE:LANE + D_ROPE] = krope[:, 0:D_ROPE]


def _prep(p, gcq, gckv, wuq, wuk, cos_t, sin_t, pos_tiles, tm):
    m = p.shape[0]
    tok = lambda w, dt: (jax.ShapeDtypeStruct((m, w), dt), pl.BlockSpec((tm, w), lambda i: (i, 0)))
    hm = lambda h, w: (jax.ShapeDtypeStruct((h, m, w), BF16), pl.BlockSpec((h, tm, w), lambda i: (0, i, 0)))
    outs = [hm(H_A // 2, LANE)] + [tok(LANE, BF16)] * 6 + [tok(LANE, F32), hm(H_I // 2, LANE), hm(H_B, LANE)] \
        + [tok(LANE, BF16)] * 3 + [hm(H_C, 2 * LANE), tok(2 * LANE, BF16),
                                   tok(CACHE_A, F32), tok(CACHE_B, F32), tok(CACHE_C, F32)]
    full = lambda a: pl.BlockSpec(a.shape, lambda i: (0,) * a.ndim)
    tab = pl.BlockSpec((tm, LANE), lambda i: (i % pos_tiles, 0))
    return pl.pallas_call(
        _prep_body, grid=(m // tm,),
        in_specs=[pl.BlockSpec((tm, p.shape[1]), lambda i: (i, 0)), full(gcq), full(gckv), full(wuq), full(wuk), tab, tab],
        out_specs=[o[1] for o in outs], out_shape=[o[0] for o in outs],
        compiler_params=_cparams(1), name="prep",
    )(p, gcq, gckv, wuq, wuk, cos_t, sin_t)


def _flash_init(m_ref, l_ref, acc_ref):
    m_ref[...] = jnp.full(m_ref.shape, NEG, F32)
    l_ref[...] = jnp.zeros(l_ref.shape, F32)
    acc_ref[...] = jnp.zeros(acc_ref.shape, F32)


def _flash_update(g, s, v, m_ref, l_ref, acc_ref, v_feature_major=False):
    m_old = m_ref[g]
    m_new = jnp.maximum(m_old, jnp.max(s, axis=1, keepdims=True))
    alpha = jnp.exp2(m_old - m_new)
    p = jnp.exp2(s - m_new)
    l_ref[g] = alpha * l_ref[g] + jnp.sum(p, axis=1, keepdims=True)
    pb = p.astype(BF16)
    pv = _nt(pb, v) if v_feature_major else jnp.dot(pb, v, preferred_element_type=F32)
    acc_ref[g] = alpha * acc_ref[g] + pv
    m_ref[g] = m_new


def _prompt_sweep(i, q_all, k_refs, v_of, make_far, make_near, m_ref, l_ref, acc_ref):
    n_far = jnp.maximum(i - 1, 0)
    n_heads = q_all.shape[0] // TQ

    def chunk(start, ck, add):
        for g, k_ref in enumerate(k_refs):
            s = _nt(q_all, k_ref[pl.ds(start, ck), :])
            if add is not None:
                s = jnp.concatenate([add(g, hb, s[hb * TQ:(hb + 1) * TQ]) for hb in range(n_heads)], axis=0)
            _flash_update(g, s, v_of(g, start, ck), m_ref, l_ref, acc_ref)

    def far_body(jj, carry):
        chunk(pl.multiple_of(jj * 2 * TQ, 2 * TQ), 2 * TQ, make_far(2 * jj, 2))
        return carry

    lax.fori_loop(0, n_far // 2, far_body, 0)

    @pl.when(n_far % 2 == 1)
    def _():
        chunk(pl.multiple_of((n_far - 1) * TQ, TQ), TQ, make_far(n_far - 1, 1))

    chunk(pl.multiple_of(n_far * TQ, TQ), 2 * TQ, make_near())


def _sweep_scratch(rows, halves):
    return [pltpu.VMEM((halves, rows, 1), F32), pltpu.VMEM((halves, rows, 1), F32), pltpu.VMEM((halves, rows, LANE), F32)]


def _score_keys(score, allowed=None):
    bits = lax.bitcast_convert_type(score, I32)
    key = bits ^ ((bits >> 31) & np.int32(0x7FFFFFFF))
    return key if allowed is None else jnp.where(allowed, key, INT_MIN)


def _kth_largest(count_ge, shape, k):
    def bit_body(b, t):
        cand = t ^ jnp.left_shift(np.int32(1), 31 - b)
        return jnp.where(count_ge(cand) >= k, cand, t)
    return lax.fori_loop(0, 32, bit_body, jnp.full(shape, INT_MIN, I32))


def _tri(n):
    r = lax.broadcasted_iota(I32, (n, n), 0)
    c = lax.broadcasted_iota(I32, (n, n), 1)
    return jnp.where(r <= c, 1.0, 0.0).astype(BF16)


def _select_block(key, thr, need, carry, tri):
    eq = (key == thr) & (thr != INT_MIN)
    eqf = jnp.where(eq, 1.0, 0.0)
    pref = jnp.dot(eqf.astype(BF16), tri, preferred_element_type=F32) + carry
    sel = (key > thr) | (eq & (pref <= need))
    return jnp.where(sel, 0.0, NEG), carry + jnp.sum(eqf, axis=1, keepdims=True)


def _prompt_a_body(qa_ref, qi_ref, wi_ref, ke_ref, ko_ref, ve_ref, vo_ref, kie_ref, kio_ref, bias_ref, o_ref,
                   key_ref, mask_ref, wb_ref, m_ref, l_ref, acc_ref, *, k_top):
    i = pl.program_id(1)
    n_act = i + 1
    np_i = H_I // 2
    qi_all = qi_ref[...].reshape(np_i * TQ, LANE)
    wi = wi_ref[...]
    for h in range(H_I):
        wb_ref[h] = jnp.broadcast_to(wi[:, h:h + 1], (TQ, LANE))
    row = lax.broadcasted_iota(I32, (TQ, LANE), 0)
    col = lax.broadcasted_iota(I32, (TQ, LANE), 1)

    def idx_body(j, carry):
        st = pl.multiple_of(j * LANE, LANE)
        se = _nt(qi_all, kie_ref[pl.ds(st, LANE), :])
        so = _nt(qi_all, kio_ref[pl.ds(st, LANE), :])
        tot = jnp.zeros((TQ, LANE), F32)
        for p in range(np_i):
            rows = slice(p * TQ, (p + 1) * TQ)
            tot = tot + jnp.maximum(se[rows], 0.0) * wb_ref[2 * p] + jnp.maximum(so[rows], 0.0) * wb_ref[2 * p + 1]
        key_ref[j] = _score_keys(tot, (j * LANE + col) <= (i * TQ + row))
        return carry

    lax.fori_loop(0, n_act, idx_body, 0)

    def count(pred_of):
        def body(j, c):
            return c + jnp.where(pred_of(key_ref[j]), 1.0, 0.0)
        return jnp.sum(lax.fori_loop(0, n_act, body, jnp.zeros((TQ, LANE), F32)), axis=1, keepdims=True)

    thr = _kth_largest(lambda cand: count(lambda k: k >= cand), (TQ, 1), float(k_top))
    need = float(k_top) - count(lambda k: k > thr)
    tri = _tri(LANE)

    def mask_body(j, carry):
        mask_ref[j], carry = _select_block(key_ref[j], thr, need, carry, tri)
        return carry

    lax.fori_loop(0, n_act, mask_body, jnp.zeros((TQ, 1), F32))

    @pl.when(i == 0)
    def _():
        mask_ref[1] = jnp.full((TQ, LANE), NEG, F32)

    np_a = H_A // 2
    _flash_init(m_ref, l_ref, acc_ref)
    q_all = qa_ref[...].reshape(np_a * TQ, LANE)
    v_refs = (ve_ref, vo_ref)
    var = jnp.where(i == 0, 1, 0)

    def mask_cat(blk, nb):
        return mask_ref[blk] if nb == 1 else jnp.concatenate([mask_ref[blk], mask_ref[blk + 1]], axis=1)

    def make_far(blk, nb):
        mk = mask_cat(blk, nb)
        return lambda g, pair, s: s + mk

    def make_near():
        mk = mask_cat(jnp.maximum(i - 1, 0), 2)
        return lambda g, pair, s: s + bias_ref[var, g, pair] + mk

    _prompt_sweep(i, q_all, (ke_ref, ko_ref), lambda g, st, ck: v_refs[g][pl.ds(st, ck), :],
                  make_far, make_near, m_ref, l_ref, acc_ref)
    for p in range(np_a):
        rows = slice(p * TQ, (p + 1) * TQ)
        out = acc_ref[0, rows] / l_ref[0, rows] + acc_ref[1, rows] / l_ref[1, rows]
        o_ref[:, p * LANE:(p + 1) * LANE] = out.astype(BF16)


def _prompt_a(qa, qi, wi, ke, ko, ve, vo, kie, kio, bias, n, s):
    kv = pl.BlockSpec((None, s, LANE), lambda b, i: (b, 0, 0))
    hm = lambda h: pl.BlockSpec((h, None, TQ, LANE), lambda b, i: (0, b, i, 0))
    r = (H_A // 2) * TQ
    nblk = max(s // LANE, 2)
    return pl.pallas_call(
        functools.partial(_prompt_a_body, k_top=min(TOPK_MAX, s // 4)),
        grid=(n, s // TQ),
        in_specs=[hm(H_A // 2), hm(H_I // 2), pl.BlockSpec((None, TQ, LANE), lambda b, i: (b, i, 0)),
                  kv, kv, kv, kv, kv, kv, pl.BlockSpec(bias.shape, lambda b, i: (0,) * bias.ndim)],
        out_specs=pl.BlockSpec((None, TQ, H_A * HD_A), lambda b, i: (b, i, 0)),
        out_shape=jax.ShapeDtypeStruct((n, s, H_A * HD_A), BF16),
        scratch_shapes=[pltpu.VMEM((nblk, TQ, LANE), I32), pltpu.VMEM((nblk, TQ, LANE), F32),
                        pltpu.VMEM((H_I, TQ, LANE), F32)] + _sweep_scratch(r, 2),
        compiler_params=_cparams(2), name="prompt_a",
    )(qa, qi, wi, ke, ko, ve, vo, kie, kio, bias)


def _lambda_of(lamv, lam_init):
    return (jnp.exp(jnp.sum(lamv[0:1] * lamv[1:2], axis=1, keepdims=True))
            - jnp.exp(jnp.sum(lamv[2:3] * lamv[3:4], axis=1, keepdims=True)) + lam_init)


def _no_add(blk, nb):
    return None


def _prompt_b_body(qb_ref, k1_ref, k2_ref, vb_ref, bias_ref, lam_ref, g_ref, o_ref,
                   m_ref, l_ref, acc_ref, *, lam_init):
    i = pl.program_id(1)
    _flash_init(m_ref, l_ref, acc_ref)
    q_all = qb_ref[...].reshape(H_B * TQ, LANE)
    var = jnp.where(i == 0, 1, 0)

    def make_near():
        return lambda g, head, s: s + bias_ref[var, head]

    _prompt_sweep(i, q_all, (k1_ref, k2_ref), lambda g, st, ck: vb_ref[pl.ds(st, ck), :],
                  _no_add, make_near, m_ref, l_ref, acc_ref)
    lam = _lambda_of(lam_ref[...], lam_init)
    for h in range(H_B):
        rows = slice(h * TQ, (h + 1) * TQ)
        o = acc_ref[0, rows] / l_ref[0, rows] - lam * (acc_ref[1, rows] / l_ref[1, rows])
        o_ref[:, h * LANE:(h + 1) * LANE] = (_rms(o, g_ref[...]) * (1.0 - lam_init)).astype(BF16)


def _prompt_b(qb, k1, k2, vb, bias, lamv, g, lam_init, n, s):
    kv = pl.BlockSpec((None, s, LANE), lambda b, i: (b, 0, 0))
    full = lambda a: pl.BlockSpec(a.shape, lambda b, i: (0,) * a.ndim)
    r = H_B * TQ
    return pl.pallas_call(
        functools.partial(_prompt_b_body, lam_init=lam_init),
        grid=(n, s // TQ),
        in_specs=[pl.BlockSpec((H_B, None, TQ, LANE), lambda b, i: (0, b, i, 0)), kv, kv, kv, full(bias), full(lamv), full(g)],
        out_specs=pl.BlockSpec((None, TQ, H_B * 2 * D_B), lambda b, i: (b, i, 0)),
        out_shape=jax.ShapeDtypeStruct((n, s, H_B * 2 * D_B), BF16),
        scratch_shapes=_sweep_scratch(r, 2),
        compiler_params=_cparams(2), name="prompt_b",
    )(qb, k1, k2, vb, bias, lamv, g)


def _prompt_c_body(qc_ref, rck_ref, cm_ref, wuv_ref, o_ref, m_ref, l_ref, acc_ref):
    i = pl.program_id(1)
    _flash_init(m_ref, l_ref, acc_ref)
    q_all = qc_ref[...].reshape(H_C * TQ, 2 * LANE)
    var = jnp.where(i == 0, 1, 0)

    def make_near():
        cm = cm_ref[var]
        return lambda g, head, s: s + cm

    _prompt_sweep(i, q_all, (rck_ref,), lambda g, st, ck: rck_ref[pl.ds(st, ck), 0:LANE],
                  _no_add, make_near, m_ref, l_ref, acc_ref)
    for h in range(H_C):
        rows = slice(h * TQ, (h + 1) * TQ)
        lat = (acc_ref[0, rows] / l_ref[0, rows]).astype(BF16)
        y = jnp.dot(lat, wuv_ref[h], preferred_element_type=F32)
        o_ref[:, h * LANE:(h + 1) * LANE] = y.astype(BF16)


def _prompt_c(qc, rck, cmask, wuv, n, s):
    full = lambda a: pl.BlockSpec(a.shape, lambda b, i: (0,) * a.ndim)
    r = H_C * TQ
    return pl.pallas_call(
        _prompt_c_body,
        grid=(n, s // TQ),
        in_specs=[pl.BlockSpec((H_C, None, TQ, 2 * LANE), lambda b, i: (0, b, i, 0)),
                  pl.BlockSpec((None, s, 2 * LANE), lambda b, i: (b, 0, 0)), full(cmask), full(wuv)],
        out_specs=pl.BlockSpec((None, TQ, H_C * D_VC), lambda b, i: (b, i, 0)),
        out_shape=jax.ShapeDtypeStruct((n, s, H_C * D_VC), BF16),
        scratch_shapes=_sweep_scratch(r, 1),
        compiler_params=_cparams(2), name="prompt_c",
    )(qc, rck, cmask, wuv)


def _page_specs(pps, n_pages, shape, layer):
    def spec(j):
        return pl.BlockSpec((None, None) + shape,
                            lambda b, c, pt: (layer, pt[b * n_pages + c * pps + j], 0, 0))
    return [spec(j) for j in range(pps)]


SROWS = 8


def _sample_score_body(pt_ref, *refs, pps):
    pages = refs[:pps]
    new_ref, qi_ref, w_ref, sc_ref, scnew_ref, kbuf_ref = refs[pps:]
    c = pl.program_id(1)

    def scores(kidx_t):
        s = jnp.maximum(jnp.dot(qi_ref[...], kidx_t, preferred_element_type=F32), 0.0) * w_ref[...]
        s = jnp.sum(s.reshape(DEC_SEQ, H_I, kidx_t.shape[1]), axis=1)
        return jnp.concatenate([s, jnp.full((SROWS - DEC_SEQ, kidx_t.shape[1]), -jnp.inf, F32)], axis=0)

    for j in range(pps):
        kbuf_ref[:, j * PAGE:(j + 1) * PAGE] = pages[j][2 * HD_A:CACHE_A, :].astype(BF16)
    sc = scores(kbuf_ref[...])
    for j in range(pps):
        sc_ref[j] = sc[:, j * PAGE:(j + 1) * PAGE]

    @pl.when(c == pl.num_programs(1) - 1)
    def _():
        scnew_ref[...] = scores(new_ref[2 * HD_A:CACHE_A, :].astype(BF16))


def _sample_score(pt_flat, cache_a_t, layer, new_a_t, qi_rows, w_col, n, n_pages, pps):
    per = lambda shp: pl.BlockSpec((None,) + shp, lambda b, c, pt: (b,) + (0,) * len(shp))
    grid_spec = pltpu.PrefetchScalarGridSpec(
        num_scalar_prefetch=1, grid=(n, n_pages // pps),
        in_specs=_page_specs(pps, n_pages, (CACHE_A, PAGE), layer)
        + [per((CACHE_A, PAGE)), per((DEC_SEQ * H_I, D_I)), per((DEC_SEQ * H_I, 1))],
        out_specs=[pl.BlockSpec((None, pps, SROWS, PAGE), lambda b, c, pt: (b, c, 0, 0)), per((SROWS, PAGE))],
        scratch_shapes=[pltpu.VMEM((D_I, pps * PAGE), BF16)])
    return pl.pallas_call(
        functools.partial(_sample_score_body, pps=pps), grid_spec=grid_spec, name="sample_score",
        out_shape=[jax.ShapeDtypeStruct((n, n_pages, SROWS, PAGE), F32), jax.ShapeDtypeStruct((n, SROWS, PAGE), F32)],
        compiler_params=_cparams(2),
    )(pt_flat, *([cache_a_t] * pps), new_a_t, qi_rows, w_col)


def _sample_select_body(sc_ref, scnew_ref, mask_ref, masknew_ref, key_ref, keynew_ref, *, k_top):
    g, nblk = sc_ref.shape[0], sc_ref.shape[1]
    rows = g * SROWS
    for s in range(g):
        key_ref[:, s * SROWS:(s + 1) * SROWS, :] = _score_keys(sc_ref[s])
    r = lax.broadcasted_iota(I32, (rows, PAGE), 0)
    jk = lax.broadcasted_iota(I32, (rows, PAGE), 1)
    keynew_ref[...] = _score_keys(scnew_ref[...].reshape(rows, PAGE), jk <= r % SROWS)

    def count(pred_of):
        part = jnp.sum(jnp.where(pred_of(key_ref[...]), 1.0, 0.0), axis=0) + jnp.where(pred_of(keynew_ref[...]), 1.0, 0.0)
        return jnp.sum(part, axis=1, keepdims=True)

    kf = float(k_top)
    thr = _kth_largest(lambda cand: count(lambda k: k >= cand), (rows, 1), kf)
    need = kf - count(lambda k: k > thr)
    has_tie = (thr != INT_MIN) & (count(lambda k: k >= thr) > kf)

    def plain(key):
        return jnp.where((key >= thr) & (key != INT_MIN), 0.0, NEG)

    mk_all = plain(key_ref[...])
    for s in range(g):
        mask_ref[s] = mk_all[:, s * SROWS:(s + 1) * SROWS, :]
    masknew_ref[...] = plain(keynew_ref[...]).reshape(g, SROWS, PAGE)

    @pl.when(jnp.max(jnp.where(has_tie, 1.0, 0.0)) > 0.0)
    def _():
        tri = _tri(PAGE)

        def blk_body(j, carry):
            mk, carry = _select_block(key_ref[j], thr, need, carry, tri)
            for s in range(g):
                mask_ref[s, j] = mk[s * SROWS:(s + 1) * SROWS]
            return carry

        carry = lax.fori_loop(0, nblk, blk_body, jnp.zeros((rows, 1), F32))
        mk, _ = _select_block(keynew_ref[...], thr, need, carry, tri)
        masknew_ref[...] = mk.reshape(g, SROWS, PAGE)


def _sample_select(sc, sc_new, k_top, g):
    n, nblk = sc.shape[0], sc.shape[1]
    big = pl.BlockSpec((g, nblk, SROWS, PAGE), lambda i: (i, 0, 0, 0))
    small = pl.BlockSpec((g, SROWS, PAGE), lambda i: (i, 0, 0))
    return pl.pallas_call(
        functools.partial(_sample_select_body, k_top=k_top), grid=(n // g,), name="sample_select",
        in_specs=[big, small], out_specs=[big, small],
        out_shape=[jax.ShapeDtypeStruct(sc.shape, F32), jax.ShapeDtypeStruct(sc_new.shape, F32)],
        scratch_shapes=[pltpu.VMEM((nblk, g * SROWS, PAGE), I32), pltpu.VMEM((g * SROWS, PAGE), I32)],
        compiler_params=_cparams(1),
    )(sc, sc_new)


RA_ROWS, RB_ROWS, RC_ROWS = DEC_SEQ * H_A, 2 * DEC_SEQ * H_B, DEC_SEQ * H_C


def _sample_attend_body(pt_ref, *refs, pps, lam_init):
    pa, pb, pc = refs[:pps], refs[pps:2 * pps], refs[2 * pps:3 * pps]
    (newa_ref, newb_ref, newc_ref, qa_ref, qb_ref, qc_ref, mask_ref, masknew_ref,
     ba_last_ref, ba_new_ref, bb_last_ref, bb_new_ref, cc_new_ref, lam_ref, g_ref,
     oa_ref, ob_ref, oc_ref,
     kba_ref, kbb_ref, kbc_ref, ma_ref, la_ref, acca_ref, mb_ref, lb_ref, accb_ref, mc_ref, lc_ref, accc_ref) = refs[3 * pps:]
    c = pl.program_id(1)
    n_ch = pl.num_programs(1)
    ck = pps * PAGE
    last = jnp.where(c == n_ch - 1, 1.0, 0.0)

    @pl.when(c == 0)
    def _():
        _flash_init(ma_ref, la_ref, acca_ref)
        _flash_init(mb_ref, lb_ref, accb_ref)
        _flash_init(mc_ref, lc_ref, accc_ref)
        kbc_ref[...] = jnp.zeros(kbc_ref.shape, BF16)

    for j in range(pps):
        keys = slice(j * PAGE, (j + 1) * PAGE)
        kba_ref[:, keys] = pa[j][0:LANE, :].astype(BF16)
        kbb_ref[keys, :] = pb[j][...].astype(BF16)
        kbc_ref[0:CACHE_C, keys] = pc[j][...].astype(BF16)

    def band(s, bias_ref):
        return jnp.concatenate([s[:, :ck - PAGE], s[:, ck - PAGE:] + last * bias_ref[...]], axis=1)

    def sel_rows(mask):
        return mask[0:DEC_SEQ, None, :]

    dot = lambda a, b: jnp.dot(a, b, preferred_element_type=F32)
    ka = kba_ref[...]
    sa = band(dot(qa_ref[...], ka), ba_last_ref)
    mask = jnp.concatenate([mask_ref[j] for j in range(pps)], axis=1)
    sa = (sa.reshape(DEC_SEQ, H_A, ck) + sel_rows(mask)).reshape(RA_ROWS, ck)
    _flash_update(0, sa, ka, ma_ref, la_ref, acca_ref, v_feature_major=True)
    sb = band(_nt(qb_ref[...], kbb_ref[:, 0:LANE]), bb_last_ref)
    _flash_update(0, sb, kbb_ref[:, LANE:2 * LANE], mb_ref, lb_ref, accb_ref)
    _flash_update(0, dot(qc_ref[...], kbc_ref[...]), kbc_ref[0:LANE, :], mc_ref, lc_ref, accc_ref, v_feature_major=True)

    @pl.when(c == n_ch - 1)
    def _():
        na = newa_ref[0:LANE, :].astype(BF16)
        sna = dot(qa_ref[...], na) + ba_new_ref[...]
        sna = (sna.reshape(DEC_SEQ, H_A, PAGE) + sel_rows(masknew_ref[...])).reshape(RA_ROWS, PAGE)
        _flash_update(0, sna, na, ma_ref, la_ref, acca_ref, v_feature_major=True)
        nb = newb_ref[...].astype(BF16)
        _flash_update(0, _nt(qb_ref[...], nb[:, 0:LANE]) + bb_new_ref[...], nb[:, LANE:2 * LANE], mb_ref, lb_ref, accb_ref)
        ncz = jnp.concatenate([newc_ref[...], jnp.zeros((2 * LANE - CACHE_C, PAGE), F32)], axis=0).astype(BF16)
        _flash_update(0, dot(qc_ref[...], ncz) + cc_new_ref[...], ncz[0:LANE, :], mc_ref, lc_ref, accc_ref,
                      v_feature_major=True)
        oa_ref[...] = acca_ref[0] / la_ref[0]
        ob = accb_ref[0] / lb_ref[0]
        half = RB_ROWS // 2
        o = ob[0:half] - _lambda_of(lam_ref[...], lam_init) * ob[half:RB_ROWS]
        ob_ref[...] = (_rms(o, g_ref[...]) * (1.0 - lam_init)).astype(BF16)
        oc_ref[...] = (accc_ref[0] / lc_ref[0]).astype(BF16)


def _sample_attend(pt_flat, cache_a, cache_b, cache_c, layer, new_a, new_b, new_c, qa_rows, qb_rows, qc_rows,
                   mask, mask_new, tiles, lamv, g, lam_init, n, n_pages, pps):
    n_ch = n_pages // pps
    ck = pps * PAGE
    per = lambda shp: pl.BlockSpec((None,) + shp, lambda b, c, pt: (b,) + (0,) * len(shp))
    full = lambda a: pl.BlockSpec(a.shape, lambda b, c, pt: (0,) * a.ndim)
    flash = lambda r: [pltpu.VMEM((1, r, 1), F32), pltpu.VMEM((1, r, 1), F32), pltpu.VMEM((1, r, LANE), F32)]
    grid_spec = pltpu.PrefetchScalarGridSpec(
        num_scalar_prefetch=1, grid=(n, n_ch),
        in_specs=_page_specs(pps, n_pages, (CACHE_A, PAGE), layer) + _page_specs(pps, n_pages, (PAGE, CACHE_B), layer)
        + _page_specs(pps, n_pages, (CACHE_C, PAGE), layer)
        + [per((CACHE_A, PAGE)), per((PAGE, CACHE_B)), per((CACHE_C, PAGE)),
           per((RA_ROWS, LANE)), per((RB_ROWS, LANE)), per((RC_ROWS, 2 * LANE)),
           pl.BlockSpec((None, pps, SROWS, PAGE), lambda b, c, pt: (b, c, 0, 0)), per((SROWS, PAGE))]
        + [full(t) for t in tiles] + [full(lamv), full(g)],
        out_specs=[per((RA_ROWS, LANE)), per((RB_ROWS // 2, LANE)), per((RC_ROWS, LANE))],
        scratch_shapes=[pltpu.VMEM((LANE, ck), BF16), pltpu.VMEM((ck, 2 * LANE), BF16), pltpu.VMEM((2 * LANE, ck), BF16)]
        + flash(RA_ROWS) + flash(RB_ROWS) + flash(RC_ROWS))
    return pl.pallas_call(
        functools.partial(_sample_attend_body, pps=pps, lam_init=lam_init),
        grid_spec=grid_spec, name="sample_attend",
        out_shape=[jax.ShapeDtypeStruct((n, RA_ROWS, LANE), F32), jax.ShapeDtypeStruct((n, RB_ROWS // 2, LANE), BF16),
                   jax.ShapeDtypeStruct((n, RC_ROWS, LANE), BF16)],
        compiler_params=_cparams(2),
    )(pt_flat, *([cache_a] * pps), *([cache_b] * pps), *([cache_c] * pps), new_a, new_b, new_c,
      qa_rows, qb_rows, qc_rows, mask, mask_new, *tiles, lamv, g)


def _uv_body(lat_ref, w_ref, o_ref):
    o_ref[...] = jnp.dot(lat_ref[...], w_ref[...], preferred_element_type=F32).astype(BF16)


def _uv(lat, wuv):
    h, m, _ = lat.shape
    return pl.pallas_call(
        _uv_body, grid=(h,),
        in_specs=[pl.BlockSpec((None, m, LANE), lambda i: (i, 0, 0)), pl.BlockSpec((None, LANE, LANE), lambda i: (i, 0, 0))],
        out_specs=pl.BlockSpec((None, m, LANE), lambda i: (i, 0, 0)),
        out_shape=jax.ShapeDtypeStruct((h, m, LANE), BF16), compiler_params=_cparams(1),
    )(lat, wuv)


def _merge_body(ya_ref, yb_ref, yc_ref, wa_ref, wb_ref, wc_ref, g0_ref, g1_ref, g2_ref, o_ref):
    dot = lambda a, b: jnp.dot(a[...], b[...], preferred_element_type=F32)
    o_ref[...] = (g0_ref[...].astype(F32) * dot(ya_ref, wa_ref) + g1_ref[...].astype(F32) * dot(yb_ref, wb_ref)
                  + g2_ref[...].astype(F32) * dot(yc_ref, wc_ref)).astype(BF16)


def _merge(ya, yb, yc, wa, wb, wc, gates, tm, tn):
    m = ya.shape[0]
    d = wa.shape[1]
    nt = d // tn
    xs = lambda a: pl.BlockSpec((tm, a.shape[1]), lambda j, i: (i, 0))
    ws = lambda a: pl.BlockSpec((a.shape[0], tn), lambda j, i: (0, j))
    gs = lambda k: pl.BlockSpec((tm, tn), lambda j, i: (i, k * nt + j))
    return pl.pallas_call(
        _merge_body, grid=(nt, m // tm),
        in_specs=[xs(ya), xs(yb), xs(yc), ws(wa), ws(wb), ws(wc), gs(0), gs(1), gs(2)],
        out_specs=pl.BlockSpec((tm, tn), lambda j, i: (i, j)),
        out_shape=jax.ShapeDtypeStruct((m, d), BF16), compiler_params=_cparams(2), name="merge",
    )(ya, yb, yc, wa, wb, wc, gates, gates, gates)


def _proj_resid_body(a_ref, w_ref, x_ref, gt_ref, ada_ref, o_ref):
    y = jnp.dot(a_ref[...], w_ref[...], preferred_element_type=F32)
    o_ref[...] = x_ref[...] + (gt_ref[...] + ada_ref[...]) * y


def _proj_resid(a, w, x, grp, kg, ada, tm, tn):
    m, kd = a.shape
    d = w.shape[1]
    return pl.pallas_call(
        _proj_resid_body, grid=(d // tn, m // tm),
        in_specs=[pl.BlockSpec((tm, kd), lambda j, i: (i, 0)), pl.BlockSpec((kd, tn), lambda j, i: (0, j)),
                  pl.BlockSpec((tm, tn), lambda j, i: (i, j)), grp.mod_spec(tm, tn, lambda j, i: (i, j)),
                  pl.BlockSpec((1, tn), lambda j, i: (0, j))],
        out_specs=pl.BlockSpec((tm, tn), lambda j, i: (i, j)),
        out_shape=jax.ShapeDtypeStruct((m, d), F32), compiler_params=_cparams(2), name="proj_resid",
    )(a, w, x, grp.mods[kg], ada[kg:kg + 1])


def _top_rows(x, k):
    vals = []
    for _ in range(k):
        m = jnp.max(x, axis=0, keepdims=True)
        vals.append(m)
        x = jnp.where(x == m, -jnp.inf, x)
    return jnp.concatenate(vals, axis=0)


def _peer_select_body(q_ref, k1_ref, k2_ref, s1_ref, e1_ref, s2_ref, e2_ref, thr_ref):
    thr_rows = []
    for h in range(PEER_HEADS):
        qp = q_ref[:, h * D_KEY:(h + 1) * D_KEY]
        s1 = _nt(k1_ref[h], qp)
        s2 = _nt(k2_ref[h], qp)
        a1 = _top_rows(s1, PEER_TOPK)
        a2 = _top_rows(s2, PEER_TOPK)
        cand = jnp.concatenate([a1[r:r + 1] + a2 for r in range(PEER_TOPK)], axis=0)
        best = _top_rows(cand, PEER_TOPK)
        z = jnp.sum(jnp.exp(best - best[0:1]), axis=0, keepdims=True)
        s1_ref[h] = s1
        s2_ref[h] = s2
        e1_ref[h] = jnp.exp(s1 - a1[0:1])
        e2_ref[h] = (jnp.exp(s2 - a2[0:1]) / z).astype(BF16)
        thr_rows.append(best[PEER_TOPK - 1:PEER_TOPK])
    thr_ref[...] = jnp.concatenate(thr_rows, axis=0)


def _peer_select(q, k1, k2, tm):
    m = q.shape[0]
    hk = lambda dt: (jax.ShapeDtypeStruct((PEER_HEADS, N_KEYS, m), dt),
                     pl.BlockSpec((PEER_HEADS, N_KEYS, tm), lambda i: (0, 0, i)))
    outs = [hk(F32), hk(F32), hk(F32), hk(BF16),
            (jax.ShapeDtypeStruct((PEER_HEADS, m), F32), pl.BlockSpec((PEER_HEADS, tm), lambda i: (0, i)))]
    full = lambda a: pl.BlockSpec(a.shape, lambda i: (0,) * a.ndim)
    return pl.pallas_call(
        _peer_select_body, grid=(m // tm,),
        in_specs=[pl.BlockSpec((tm, q.shape[1]), lambda i: (i, 0)), full(k1), full(k2)],
        out_specs=[o[1] for o in outs], out_shape=[o[0] for o in outs],
        compiler_params=_cparams(1), name="peer_select",
    )(q, k1, k2)


def _peer_body(h_ref, u_ref, v_ref, s1_ref, e1_ref, s2_ref, e2_ref, thr_ref, o_ref, *, eb):
    e = pl.program_id(1)

    @pl.when(e == 0)
    def _():
        o_ref[...] = jnp.zeros(o_ref.shape, F32)

    a_t = _nt(u_ref[...], h_ref[...])
    act = 0.5 * a_t * (1.0 + lax.erf(a_t * (2.0 ** -0.5)))
    rows = []
    for il in range(eb // N_KEYS):
        g = jnp.zeros((N_KEYS, a_t.shape[1]), F32)
        for h in range(PEER_HEADS):
            ssum = s1_ref[il, h:h + 1, :] + s2_ref[h]
            gate = e1_ref[il, h:h + 1, :] * e2_ref[h].astype(F32)
            g = g + jnp.where(ssum >= thr_ref[h:h + 1, :], gate, 0.0)
        rows.append(g * act[il * N_KEYS:(il + 1) * N_KEYS])
    ga = jnp.concatenate(rows, axis=0).T.astype(BF16)
    o_ref[...] += jnp.dot(ga, v_ref[...], preferred_element_type=F32)


def _peer(h, u, v, s1, e1, s2, e2, thr, tm, eb):
    m, d = h.shape
    ne = u.shape[0] // eb
    sel = lambda: pl.BlockSpec((PEER_HEADS, N_KEYS, tm), lambda i, e: (0, 0, i))
    rows = lambda: pl.BlockSpec((eb // N_KEYS, PEER_HEADS, tm), lambda i, e: (e, 0, i))
    return pl.pallas_call(
        functools.partial(_peer_body, eb=eb), grid=(m // tm, ne), name="peer",
        in_specs=[pl.BlockSpec((tm, d), lambda i, e: (i, 0)), pl.BlockSpec((eb, d), lambda i, e: (e, 0)),
                  pl.BlockSpec((eb, d), lambda i, e: (e, 0)), rows(), rows(), sel(), sel(),
                  pl.BlockSpec((PEER_HEADS, tm), lambda i, e: (0, i))],
        out_specs=pl.BlockSpec((tm, d), lambda i, e: (i, 0)),
        out_shape=jax.ShapeDtypeStruct((m, d), F32),
        compiler_params=_cparams(2),
    )(h, u, v, s1, e1, s2, e2, thr)


def _rel_bucket(dist):
    n = jnp.maximum(dist, 0)
    max_exact = N_BUCKETS // 2
    nf = jnp.maximum(n, 1).astype(F32)
    large = max_exact + (jnp.log(nf / max_exact) / math.log(MAX_DIST / max_exact) * (N_BUCKETS - max_exact)).astype(I32)
    return jnp.where(n < max_exact, n, jnp.minimum(large, N_BUCKETS - 1))


def _bias_of_dist(rel_bias, dist):
    tab = (rel_bias - rel_bias[N_BUCKETS - 1:N_BUCKETS]) * LOG2E
    return jnp.where((dist >= 0)[..., None], tab[_rel_bucket(dist)], NEG)


def _prompt_bias_tiles(rel_bias):
    t = jnp.arange(TQ)[:, None]
    s = jnp.arange(2 * TQ)[None, :]
    d_mid = t + TQ - s
    d_first = jnp.where(s < TQ, t - s, -1)
    tiles = jnp.stack([_bias_of_dist(rel_bias, d_mid), _bias_of_dist(rel_bias, d_first)])
    tiles = jnp.moveaxis(tiles, -1, 1)
    ta = tiles[:, :H_A].reshape(2, H_A // 2, 2, TQ, 2 * TQ).swapaxes(1, 2)
    tb = tiles[:, H_A:]
    cm = jnp.stack([jnp.where(d_mid >= 0, 0.0, NEG), jnp.where(d_first >= 0, 0.0, NEG)]).astype(F32)
    return ta, tb, cm


def _sample_bias_tiles(rel_bias, past):
    t = jnp.arange(DEC_SEQ)
    s_last = past - PAGE + jnp.arange(PAGE)
    d_last = past + t[:, None] - s_last[None, :]
    j = jnp.arange(PAGE)
    d_new = jnp.where(j[None, :] < DEC_SEQ, t[:, None] - j[None, :], -1)
    b_last = _bias_of_dist(rel_bias, d_last)
    b_new = _bias_of_dist(rel_bias, d_new)
    th = lambda b, h0, h1: jnp.moveaxis(b[..., h0:h1], -1, 1).reshape(-1, PAGE)
    ba_last, ba_new = th(b_last, 0, H_A), th(b_new, 0, H_A)
    bb_last = jnp.tile(th(b_last, H_A, H_A + H_B), (2, 1))
    bb_new = jnp.tile(th(b_new, H_A, H_A + H_B), (2, 1))
    cc_new = jnp.repeat(jnp.where(d_new >= 0, 0.0, NEG).astype(F32), H_C, axis=0)
    return [ba_last, ba_new, bb_last, bb_new, cc_new]


def _rope_tables(pos):
    half = D_ROPE // 2
    inv_freq = ROPE_BASE ** (-jnp.arange(half, dtype=F32) / half)
    ang = pos.astype(F32)[:, None] * inv_freq
    cos, sin = jnp.cos(ang), jnp.sin(ang)
    pad = jnp.zeros((pos.shape[0], LANE - D_ROPE), F32)
    return jnp.concatenate([cos, cos, pad], axis=1), jnp.concatenate([-sin, sin, pad], axis=1)


def _swap_halves(w):
    h = w.shape[-1] // 2
    return jnp.concatenate([w[..., h:], w[..., :h]], axis=-1)


def _layer_weights(l, d, w_in, w_uq, w_uk, w_uv):
    splits = (H_A * HD_A, 2 * HD_A, H_I * D_I, H_I, D_I, H_B * 2 * D_B, 2 * D_B, 2 * D_B, Q_LORA, R_KV, D_ROPE, 3 * d)
    cs = np.cumsum((0,) + splits)
    qa, kva, qi, wi, ki, qb, kb, vb, cq, ckv, kr, gt = [w_in[l][:, cs[k]:cs[k + 1]] for k in range(12)]
    z = lambda n: jnp.zeros((d, n), F32)
    k_a, v_a, k1, k2 = kva[:, :HD_A], kva[:, HD_A:], kb[:, :D_B], kb[:, D_B:]
    main = jnp.concatenate([
        qa, k_a, z(64), z(64), k_a, v_a, z(64), z(64), v_a, ki, z(64), wi, z(32), ki, qi, qb,
        k1, z(64), z(64), k2, vb, cq, ckv, kr, z(LANE - D_ROPE), _swap_halves(kr), z(LANE - D_ROPE)], axis=1)
    assert main.shape[1] == N_MAIN_BLOCKS * LANE
    uq = w_uq[l].reshape(Q_LORA, H_C, D_NOPE + D_ROPE)
    rope_w = uq[:, :, D_NOPE:]
    padr = lambda w: jnp.concatenate([w, jnp.zeros((Q_LORA, H_C, LANE - D_ROPE), F32)], axis=-1).reshape(Q_LORA, H_C * LANE)
    uq_ext = jnp.concatenate([uq[:, :, :D_NOPE].reshape(Q_LORA, H_C * D_NOPE), padr(rope_w), padr(_swap_halves(rope_w))], axis=1)
    return (main.astype(BF16), gt.astype(BF16), uq_ext.astype(BF16),
            jnp.transpose(w_uk[l], (1, 2, 0)).astype(BF16), jnp.transpose(w_uv[l], (1, 0, 2)).astype(BF16))


def _tile(m, cap):
    t = min(m, cap)
    assert m % t == 0
    return t


def _project(x, grp, l, wts, g_mix, ada, g_cq, g_ckv, cos_t, sin_t, pos_tiles, tm_prep):
    w_main, w_gate, uq_ext, ukT, _ = wts
    m = x.shape[0]
    h = _modulate(x, g_mix[l], grp, 1, 0, ada, _tile(m, 256))
    tm = _tile(m, 1024)
    p = _mm(h, w_main, tm, 768, F32)
    gates = _mm(h, w_gate, tm, 512, BF16, act="sigmoid")
    outs = _prep(p, g_cq[l].reshape(1, -1), g_ckv[l].reshape(1, -1), uq_ext, ukT, cos_t, sin_t, pos_tiles, tm_prep)
    return outs, gates


def _peer_ffn(x, grp, l, ada, g_ffn, wq, k1p, k2p, u, v):
    m, d = x.shape
    h = _modulate(x, g_ffn[l], grp, 4, 3, ada, _tile(m, 256))
    q = _mm(h, wq, _tile(m, 1024), 512, BF16)
    tm = _tile(m, 512)
    s1, e1, s2, e2, thr = _peer_select(q, k1p, k2p, _tile(m, 256))
    y = _peer(h, u, v, jnp.swapaxes(s1, 0, 1), jnp.swapaxes(e1, 0, 1), s2, e2, thr, tm, 512)
    return _resid(x, y, grp, 5, ada, _tile(m, 256))


def kernel(x_prompt, x_sample, cache_a, cache_b, cache_c, page_table, c_prompt, c_sample, w_ada, ada_emb, g_mix, w_in, g_cq, w_uq, g_ckv, w_uk, w_uv, lam_q1, lam_k1, lam_q2, lam_k2, g_subln, rel_bias, w_pa, w_pb, w_pc, w_o, g_ffn, peer_wq, peer_keys, peer_u, peer_v, g_final):
    nb, seq, d = x_prompt.shape
    ns, ts, _ = x_sample.shape
    assert ts == DEC_SEQ and seq % (2 * TQ) == 0
    depth = w_in.shape[0]
    n_pages = page_table.shape[1]
    past = n_pages * PAGE
    pps = min(16, n_pages)
    mp, ms = nb * seq, ns * ts

    c_all = jnp.concatenate([c_prompt, c_sample], axis=0)
    mod = _mm(c_all, w_ada, nb + ns, 512, F32, pre="silu").reshape(nb + ns, 6, d)
    grp_p = _Group(mp, [mod[:nb, k].reshape(nb, 1, d) for k in range(6)], seq)
    grp_s = _Group(ms, [jnp.repeat(mod[nb:, k], ts, axis=0) for k in range(6)], None)

    cos_p, sin_p = _rope_tables(jnp.arange(seq))
    cos_s, sin_s = _rope_tables(jnp.tile(past + jnp.arange(ts), ns))
    bias_a, bias_b, cmask = _prompt_bias_tiles(rel_bias)
    s_tiles = _sample_bias_tiles(rel_bias, past)
    pt_flat = page_table.reshape(-1)

    cache_a_t = jnp.swapaxes(cache_a, 2, 3)
    cache_c_t = jnp.swapaxes(cache_c, 2, 3)

    xp = x_prompt.reshape(mp, d)
    xs = x_sample.reshape(ms, d)
    rows = [[] for _ in range(6)]
    tmp_prep = _tile(seq, 256)
    tms_prep = _tile(ms, 256)
    for l in range(depth):
        lam_init = 0.8 - 0.6 * math.exp(-0.3 * l)
        lamv = jnp.stack([lam_q1[l], lam_k1[l], lam_q2[l], lam_k2[l]]).astype(F32)
        gsub = g_subln[l].reshape(1, 2 * D_B)
        ada = ada_emb[l]
        wts = _layer_weights(l, d, w_in, w_uq, w_uk, w_uv)
        wuv = wts[4]
        wpa, wpb, wpc, wo = (w.astype(BF16) for w in (w_pa[l], w_pb[l], w_pc[l], w_o[l]))

        (qa, ke, ko, ve, vo, kie, kio, wi, qi, qb, k1, k2, vb, qc, rck, ra, rb, rc), gates = _project(
            xp, grp_p, l, wts, g_mix, ada, g_cq, g_ckv, cos_p, sin_p, seq // tmp_prep, tmp_prep)
        s3 = lambda a: a.reshape(nb, seq, a.shape[-1])
        h4 = lambda a: a.reshape(a.shape[0], nb, seq, a.shape[-1])
        ya = _prompt_a(h4(qa), h4(qi), s3(wi), s3(ke), s3(ko), s3(ve), s3(vo), s3(kie), s3(kio), bias_a, nb, seq)
        yb = _prompt_b(h4(qb), s3(k1), s3(k2), s3(vb), bias_b, lamv, gsub, lam_init, nb, seq)
        yc = _prompt_c(h4(qc), s3(rck), cmask, wuv, nb, seq)
        tm = _tile(seq, 1024)
        merged = _merge(ya.reshape(mp, -1), yb.reshape(mp, -1), yc.reshape(mp, -1), wpa, wpb, wpc, gates, tm, 512)
        xp = _proj_resid(merged, wo, xp, grp_p, 2, ada, tm, 512)
        for k, r in zip((0, 2, 4), (ra, rb, rc)):
            rows[k].append(r.reshape(nb, seq, -1))

        (qa, ke, ko, ve, vo, kie, kio, wi, qi, qb, k1, k2, vb, qc, rck, ra, rb, rc), gates = _project(
            xs, grp_s, l, wts, g_mix, ada, g_cq, g_ckv, cos_s, sin_s, ms // tms_prep, tms_prep)
        pad_page = lambda r: jnp.pad(r.reshape(ns, ts, -1), ((0, 0), (0, PAGE - ts), (0, 0)))
        new_a, new_b, new_c = jnp.swapaxes(pad_page(ra), 1, 2), pad_page(rb), jnp.swapaxes(pad_page(rc), 1, 2)
        heads = lambda a, w: jnp.moveaxis(a, 0, 1).reshape(ns, ts, -1, w)
        qi_rows = heads(qi, D_I).reshape(ns, ts * H_I, D_I)
        w_col = wi.reshape(ns, ts, LANE)[:, :, :H_I].reshape(ns, ts * H_I, 1)
        sc, sc_new = _sample_score(pt_flat, cache_a_t, l, new_a, qi_rows, w_col, ns, n_pages, pps)
        mask, mask_new = _sample_select(sc, sc_new, min(TOPK_MAX, (past + ts) // 4), math.gcd(ns, 8))
        qa_rows = jnp.pad(heads(qa, HD_A).reshape(ns, RA_ROWS, HD_A), ((0, 0), (0, 0), (0, LANE - HD_A)))
        qb4 = heads(qb, D_B).reshape(ns, ts, H_B, 2, D_B)
        zq = jnp.zeros((ns, ts, H_B, D_B), BF16)
        qb_rows = jnp.concatenate([jnp.concatenate([qb4[..., 0, :], zq], axis=-1).reshape(ns, ts * H_B, LANE),
                                   jnp.concatenate([zq, qb4[..., 1, :]], axis=-1).reshape(ns, ts * H_B, LANE)], axis=1)
        qc_rows = jnp.moveaxis(qc, 0, 1).reshape(ns, RC_ROWS, 2 * LANE)
        oa, ob, oc = _sample_attend(pt_flat, cache_a_t, cache_b, cache_c_t, l, new_a, new_b, new_c, qa_rows, qb_rows, qc_rows,
                                    mask, mask_new, s_tiles, lamv, gsub, lam_init, ns, n_pages, pps)
        ya = oa[:, :, HD_A:].astype(BF16).reshape(ms, H_A * HD_A)
        yb = ob.reshape(ms, H_B * 2 * D_B)
        lat = jnp.moveaxis(oc.reshape(ms, H_C, LANE), 1, 0)
        yc = jnp.moveaxis(_uv(lat, wuv), 0, 1).reshape(ms, H_C * D_VC)
        merged = _merge(ya, yb, yc, wpa, wpb, wpc, gates, ms, 512)
        xs = _proj_resid(merged, wo, xs, grp_s, 2, ada, ms, 512)
        for k, r in zip((1, 3, 5), (ra, rb, rc)):
            rows[k].append(r.reshape(ns, ts, -1))

        wq = peer_wq[l].astype(BF16)
        zk = jnp.zeros((PEER_HEADS, N_KEYS, D_KEY // 2), F32)
        k1p = jnp.concatenate([peer_keys[l, :, 0], zk], axis=-1).astype(BF16)
        k2p = jnp.concatenate([zk, peer_keys[l, :, 1]], axis=-1).astype(BF16)
        u, v = peer_u[l].astype(BF16), peer_v[l].astype(BF16)
        xp = _peer_ffn(xp, grp_p, l, ada, g_ffn, wq, k1p, k2p, u, v)
        xs = _peer_ffn(xs, grp_s, l, ada, g_ffn, wq, k1p, k2p, u, v)

    y_prompt = _final_norm(xp, g_final, _tile(mp, 256)).reshape(nb, seq, d)
    y_sample = _final_norm(xs, g_final, _tile(ms, 256)).reshape(ns, ts, d)
    return (y_prompt, y_sample) + tuple(jnp.stack(r) for r in rows)
```

```python
import functools
import math

import numpy as np
import jax
import jax.numpy as jnp
from jax import lax
from jax.experimental import pallas as pl
from jax.experimental.pallas import tpu as pltpu

F32 = jnp.float32
BF16 = jnp.bfloat16
I32 = jnp.int32

H_A, HD_A, H_I, D_I, TOPK_MAX = 24, 64, 32, 64, 256
H_B, D_B = 8, 64
H_C, D_NOPE, D_ROPE, D_VC, Q_LORA, R_KV = 12, 128, 32, 128, 768, 128
ROPE_BASE = 10000.0
N_BUCKETS, MAX_DIST = 32, 128
PEER_HEADS, N_KEYS, D_KEY, PEER_TOPK = 8, 128, 128, 16
PAGE = 128
DEC_SEQ = 4
EPS = 1e-6
CACHE_A, CACHE_B, CACHE_C = 2 * HD_A + D_I, 4 * D_B, R_KV + D_ROPE

LANE = 128
NEG = -1e30
INT_MIN = np.int32(-2 ** 31)
LOG2E = 1.4426950408889634
TQ = 128
N_MAIN_BLOCKS = 54
VMEM_MB = 56


def _cparams(n_axes, vmem_mb=VMEM_MB):
    return pltpu.CompilerParams(dimension_semantics=("arbitrary",) * n_axes,
                                vmem_limit_bytes=vmem_mb * 1024 * 1024)


def _nt(a, b):
    return lax.dot_general(a, b, (((1,), (1,)), ((), ())), preferred_element_type=F32)


def _rms(x, g):
    return x * lax.rsqrt(jnp.mean(x * x, axis=-1, keepdims=True) + EPS) * g


def _mm_body(x_ref, w_ref, o_ref, *, act, pre):
    x = x_ref[...]
    if pre == "silu":
        x = (x * jax.nn.sigmoid(x)).astype(BF16)
    acc = jnp.dot(x, w_ref[...].astype(BF16), preferred_element_type=F32)
    if act == "sigmoid":
        acc = jax.nn.sigmoid(acc)
    o_ref[...] = acc.astype(o_ref.dtype)


def _mm(x, w, tm, tn, out_dtype, act=None, pre=None):
    m, kd = x.shape
    n = w.shape[1]
    return pl.pallas_call(
        functools.partial(_mm_body, act=act, pre=pre),
        grid=(n // tn, m // tm),
        in_specs=[pl.BlockSpec((tm, kd), lambda j, i: (i, 0)),
                  pl.BlockSpec((kd, tn), lambda j, i: (0, j))],
        out_specs=pl.BlockSpec((tm, tn), lambda j, i: (i, j)),
        out_shape=jax.ShapeDtypeStruct((m, n), out_dtype),
        compiler_params=_cparams(2), name="mm_%s" % (act or pre or "plain"),
    )(x, w)


class _Group:
    def __init__(self, m, mods, rows_per_mod):
        self.m = m
        self.mods = mods
        self.rows_per_mod = rows_per_mod

    def mod_spec(self, tm, tn, ij):
        if self.rows_per_mod is None:
            return pl.BlockSpec((tm, tn), lambda *g: ij(*g))
        r = self.rows_per_mod
        return pl.BlockSpec((None, 1, tn), lambda *g: ((ij(*g)[0] * tm) // r, 0, ij(*g)[1]))


def _modulate_body(x_ref, g_ref, sc_ref, sh_ref, asc_ref, ash_ref, o_ref):
    y = _rms(x_ref[...], g_ref[...])
    o_ref[...] = (y * (1.0 + sc_ref[...] + asc_ref[...]) + sh_ref[...] + ash_ref[...]).astype(BF16)


def _modulate(x, g, grp, ks, kb, ada, tm):
    m, d = x.shape
    ij = lambda i: (i, 0)
    row = pl.BlockSpec((1, d), lambda i: (0, 0))
    return pl.pallas_call(
        _modulate_body,
        grid=(m // tm,),
        in_specs=[pl.BlockSpec((tm, d), lambda i: (i, 0)), row,
                  grp.mod_spec(tm, d, ij), grp.mod_spec(tm, d, ij), row, row],
        out_specs=pl.BlockSpec((tm, d), lambda i: (i, 0)),
        out_shape=jax.ShapeDtypeStruct((m, d), BF16),
        compiler_params=_cparams(1),
    )(x, g.reshape(1, d), grp.mods[ks], grp.mods[kb], ada[ks:ks + 1], ada[kb:kb + 1])


def _resid_body(x_ref, y_ref, gt_ref, ada_ref, o_ref):
    o_ref[...] = x_ref[...] + (gt_ref[...] + ada_ref[...]) * y_ref[...]


def _resid(x, y, grp, kg, ada, tm):
    m, d = x.shape
    ij = lambda i: (i, 0)
    blk = pl.BlockSpec((tm, d), lambda i: (i, 0))
    return pl.pallas_call(
        _resid_body,
        grid=(m // tm,),
        in_specs=[blk, blk, grp.mod_spec(tm, d, ij), pl.BlockSpec((1, d), lambda i: (0, 0))],
        out_specs=blk,
        out_shape=jax.ShapeDtypeStruct((m, d), F32),
        compiler_params=_cparams(1),
    )(x, y, grp.mods[kg], ada[kg:kg + 1])


def _final_norm_body(x_ref, g_ref, o_ref):
    o_ref[...] = _rms(x_ref[...], g_ref[...])


def _final_norm(x, g, tm):
    m, d = x.shape
    blk = pl.BlockSpec((tm, d), lambda i: (i, 0))
    return pl.pallas_call(
        _final_norm_body, grid=(m // tm,),
        in_specs=[blk, pl.BlockSpec((1, d), lambda i: (0, 0))], out_specs=blk,
        out_shape=jax.ShapeDtypeStruct((m, d), F32), compiler_params=_cparams(1),
    )(x, g.reshape(1, d))


B_QA, B_KE, B_KO, B_VE, B_VO, B_KI1, B_KI2, B_QI, B_QB = 0, 12, 13, 14, 15, 16, 17, 18, 34
B_K1, B_K2, B_VB, B_CQ, B_CKV, B_KR, B_KRS = 42, 43, 44, 45, 51, 52, 53


def _prep_body(p_ref, gcq_ref, gckv_ref, wuq_ref, wuk_ref, cos_ref, sin_ref,
               qa_ref, ke_ref, ko_ref, kie_ref, kio_ref, wi_ref, qi_ref, qb_ref,
               k1_ref, k2_ref, qc_ref, rck_ref, ra_ref, rb_ref, rc_ref, *vt_refs):
    blk = lambda b, n=1: p_ref[:, b * LANE:(b + n) * LANE]
    sa = HD_A ** -0.5 * LOG2E
    for p in range(H_A // 2):
        qa_ref[p] = (blk(B_QA + p) * sa).astype(BF16)
    ke, ko, ve, vo = blk(B_KE), blk(B_KO), blk(B_VE), blk(B_VO)
    ke_ref[...] = ke.astype(BF16)
    ko_ref[...] = ko.astype(BF16)
    ki1, ki2 = blk(B_KI1), blk(B_KI2)
    lane = lax.broadcasted_iota(I32, ki2.shape, 1)
    kie_ref[...] = ki1.astype(BF16)
    kio_ref[...] = jnp.where(lane >= D_I, ki2, 0.0).astype(BF16)
    wi_ref[...] = ki2
    for p in range(H_I // 2):
        qi_ref[p] = blk(B_QI + p).astype(BF16)
    sb = D_B ** -0.5 * LOG2E
    for p in range(H_B):
        qb_ref[p] = (blk(B_QB + p) * sb).astype(BF16)
    k1, k2, vb = blk(B_K1), blk(B_K2), blk(B_VB)
    k1_ref[...] = k1.astype(BF16)
    k2_ref[...] = k2.astype(BF16)
    ra_ref[:, 0:LANE] = ke + vo
    ra_ref[:, LANE:LANE + D_I] = ki1[:, 0:D_I]
    rb_ref[:, 0:LANE] = k1 + k2
    rb_ref[:, LANE:2 * LANE] = vb
    cqn = _rms(blk(B_CQ, Q_LORA // LANE), gcq_ref[...]).astype(BF16)
    qc = jnp.dot(cqn, wuq_ref[...], preferred_element_type=F32)
    cos, sin = cos_ref[...], sin_ref[...]
    sc = (D_NOPE + D_ROPE) ** -0.5 * LOG2E
    nb = H_C
    for h in range(H_C):
        nope = qc[:, h * LANE:(h + 1) * LANE].astype(BF16)
        qlat = jnp.dot(nope, wuk_ref[h], preferred_element_type=F32)
        rp = qc[:, (nb + h) * LANE:(nb + h + 1) * LANE] * cos + qc[:, (2 * nb + h) * LANE:(2 * nb + h + 1) * LANE] * sin
        qc_ref[h, :, 0:LANE] = (qlat * sc).astype(BF16)
        qc_ref[h, :, LANE:2 * LANE] = (rp * sc).astype(BF16)
    ckvn = _rms(blk(B_CKV), gckv_ref[...])
    krope = blk(B_KR) * cos + blk(B_KRS) * sin
    rck_ref[:, 0:LANE] = ckvn.astype(BF16)
    rck_ref[:, LANE:2 * LANE] = krope.astype(BF16)
    rc_ref[:, 0:LANE] = ckvn
    rc_ref[:, LANE:LANE + D_ROPE] = krope[:, 0:D_ROPE]
    if vt_refs:
        for ref, val in zip(vt_refs, (ve, vo, vb, ckvn)):
            for j in range(val.shape[0] // LANE):
                ref[j] = val[j * LANE:(j + 1) * LANE].T.astype(BF16)


def _prep(p, gcq, gckv, wuq, wuk, cos_t, sin_t, pos_tiles, tm, feature_major_values):
    m = p.shape[0]
    tok = lambda w, dt: (jax.ShapeDtypeStruct((m, w), dt), pl.BlockSpec((tm, w), lambda i: (i, 0)))
    hm = lambda h, w: (jax.ShapeDtypeStruct((h, m, w), BF16), pl.BlockSpec((h, tm, w), lambda i: (0, i, 0)))
    vt = (jax.ShapeDtypeStruct((m // LANE, LANE, LANE), BF16), pl.BlockSpec((tm // LANE, LANE, LANE), lambda i: (i, 0, 0)))
    outs = [hm(H_A // 2, LANE)] + [tok(LANE, BF16)] * 4 + [tok(LANE, F32), hm(H_I // 2, LANE), hm(H_B, LANE)] \
        + [tok(LANE, BF16)] * 2 + [hm(H_C, 2 * LANE), tok(2 * LANE, BF16),
                                   tok(CACHE_A, F32), tok(CACHE_B, F32), tok(CACHE_C, F32)] \
        + ([vt] * 4 if feature_major_values else [])
    full = lambda a: pl.BlockSpec(a.shape, lambda i: (0,) * a.ndim)
    tab = pl.BlockSpec((tm, LANE), lambda i: (i % pos_tiles, 0))
    return pl.pallas_call(
        _prep_body, grid=(m // tm,),
        in_specs=[pl.BlockSpec((tm, p.shape[1]), lambda i: (i, 0)), full(gcq), full(gckv), full(wuq), full(wuk), tab, tab],
        out_specs=[o[1] for o in outs], out_shape=[o[0] for o in outs],
        compiler_params=_cparams(1), name="prep",
    )(p, gcq, gckv, wuq, wuk, cos_t, sin_t)


def _flash_init(m_ref, l_ref, acc_ref):
    m_ref[...] = jnp.full(m_ref.shape, NEG, F32)
    l_ref[...] = jnp.zeros(l_ref.shape, F32)
    acc_ref[...] = jnp.zeros(acc_ref.shape, F32)


def _flash_update(g, s, v, m_ref, l_ref, acc_ref, v_feature_major=False):
    m_old = m_ref[g]
    m_new = jnp.maximum(m_old, jnp.max(s, axis=1, keepdims=True))
    alpha = jnp.exp2(m_old - m_new)
    p = jnp.exp2(s - m_new)
    l_ref[g] = alpha * l_ref[g] + jnp.sum(p, axis=1, keepdims=True)
    pb = p.astype(BF16)
    pv = _nt(pb, v) if v_feature_major else jnp.dot(pb, v, preferred_element_type=F32)
    acc_ref[g] = alpha * acc_ref[g] + pv
    m_ref[g] = m_new


def _flash_update_t(g, s, vt, m_ref, l_ref, acc_ref):
    m_old = m_ref[g]
    m_new = jnp.maximum(m_old, jnp.max(s, axis=0, keepdims=True))
    alpha = jnp.exp2(m_old - m_new)
    p = jnp.exp2(s - m_new)
    l_ref[g] = alpha * l_ref[g] + jnp.sum(p, axis=0, keepdims=True)
    acc_ref[g] = alpha * acc_ref[g] + jnp.dot(vt, p.astype(BF16), preferred_element_type=F32)
    m_ref[g] = m_new


def _prompt_sweep(i, q_all, k_refs, vt_refs, make_far, make_near, m_ref, l_ref, acc_ref):
    n_far = jnp.maximum(i - 1, 0)

    def chunk(blk, nb, add):
        start = pl.multiple_of(blk * TQ, TQ)
        for g, k_ref in enumerate(k_refs):
            s = _nt(k_ref[pl.ds(start, nb * TQ), :], q_all)
            if add is not None:
                s = add(g, s)
            vt = vt_refs[g][blk] if nb == 1 else jnp.concatenate([vt_refs[g][blk], vt_refs[g][blk + 1]], axis=1)
            _flash_update_t(g, s, vt, m_ref, l_ref, acc_ref)

    def far_body(jj, carry):
        chunk(2 * jj, 2, make_far(2 * jj, 2))
        return carry

    lax.fori_loop(0, n_far // 2, far_body, 0)

    @pl.when(n_far % 2 == 1)
    def _():
        chunk(n_far - 1, 1, make_far(n_far - 1, 1))

    chunk(n_far, 2, make_near())


def _sweep_scratch(cols, halves):
    return [pltpu.VMEM((halves, 1, cols), F32), pltpu.VMEM((halves, 1, cols), F32), pltpu.VMEM((halves, LANE, cols), F32)]


def _score_keys(score, allowed=None):
    bits = lax.bitcast_convert_type(score, I32)
    key = bits ^ ((bits >> 31) & np.int32(0x7FFFFFFF))
    return key if allowed is None else jnp.where(allowed, key, INT_MIN)


def _kth_largest(count_ge, shape, k):
    def bit_body(b, t):
        cand = t ^ jnp.left_shift(np.int32(1), 31 - b)
        return jnp.where(count_ge(cand) >= k, cand, t)
    return lax.fori_loop(0, 32, bit_body, jnp.full(shape, INT_MIN, I32))


def _tri(n, lower=False):
    r = lax.broadcasted_iota(I32, (n, n), 0)
    c = lax.broadcasted_iota(I32, (n, n), 1)
    return jnp.where((r >= c) if lower else (r <= c), 1.0, 0.0).astype(BF16)


def _select_block(key, thr, need, carry, tri, keys_on_sublanes=False):
    eq = (key == thr) & (thr != INT_MIN)
    eqf = jnp.where(eq, 1.0, 0.0)
    if keys_on_sublanes:
        pref = jnp.dot(tri, eqf.astype(BF16), preferred_element_type=F32) + carry
        carry = carry + jnp.sum(eqf, axis=0, keepdims=True)
    else:
        pref = jnp.dot(eqf.astype(BF16), tri, preferred_element_type=F32) + carry
        carry = carry + jnp.sum(eqf, axis=1, keepdims=True)
    sel = (key > thr) | (eq & (pref <= need))
    return jnp.where(sel, 0.0, NEG), carry


def _prompt_a_body(qa_ref, qi_ref, wi_ref, ke_ref, ko_ref, vet_ref, vot_ref, kie_ref, kio_ref, bias_ref, o_ref,
                   key_ref, mask_ref, m_ref, l_ref, acc_ref, *, k_top):
    i = pl.program_id(1)
    n_act = i + 1
    np_i = H_I // 2
    qi_all = qi_ref[...].reshape(np_i * TQ, LANE)
    w_t = wi_ref[...].T
    krow = lax.broadcasted_iota(I32, (LANE, TQ), 0)
    qcol = lax.broadcasted_iota(I32, (LANE, TQ), 1)

    def idx_body(j, carry):
        st = pl.multiple_of(j * LANE, LANE)
        se = _nt(kie_ref[pl.ds(st, LANE), :], qi_all)
        so = _nt(kio_ref[pl.ds(st, LANE), :], qi_all)
        tot = jnp.zeros((LANE, TQ), F32)
        for p in range(np_i):
            cols = slice(p * TQ, (p + 1) * TQ)
            tot = (tot + jnp.maximum(se[:, cols], 0.0) * w_t[2 * p:2 * p + 1, :]
                   + jnp.maximum(so[:, cols], 0.0) * w_t[2 * p + 1:2 * p + 2, :])
        key_ref[j] = _score_keys(tot, (j * LANE + krow) <= (i * TQ + qcol))
        return carry

    lax.fori_loop(0, n_act, idx_body, 0)

    def count(pred_of):
        def body(j, c):
            return c + jnp.where(pred_of(key_ref[j]), 1.0, 0.0)
        return jnp.sum(lax.fori_loop(0, n_act, body, jnp.zeros((LANE, TQ), F32)), axis=0, keepdims=True)

    thr = _kth_largest(lambda cand: count(lambda k: k >= cand), (1, TQ), float(k_top))
    need = float(k_top) - count(lambda k: k > thr)
    tri = _tri(LANE, lower=True)

    def mask_body(j, carry):
        mask_ref[j], carry = _select_block(key_ref[j], thr, need, carry, tri, keys_on_sublanes=True)
        return carry

    lax.fori_loop(0, n_act, mask_body, jnp.zeros((1, TQ), F32))

    @pl.when(i == 0)
    def _():
        mask_ref[1] = jnp.full((LANE, TQ), NEG, F32)

    np_a = H_A // 2
    _flash_init(m_ref, l_ref, acc_ref)
    q_all = qa_ref[...].reshape(np_a * TQ, LANE)
    var = jnp.where(i == 0, 1, 0)

    def mask_cols(blk, nb):
        mk = mask_ref[blk] if nb == 1 else jnp.concatenate([mask_ref[blk], mask_ref[blk + 1]], axis=0)
        return jnp.concatenate([mk] * np_a, axis=1)

    def make_far(blk, nb):
        mk = mask_cols(blk, nb)
        return lambda g, s: s + mk

    def make_near():
        mk = mask_cols(jnp.maximum(i - 1, 0), 2)
        return lambda g, s: s + bias_ref[var, g] + mk

    _prompt_sweep(i, q_all, (ke_ref, ko_ref), (vet_ref, vot_ref), make_far, make_near, m_ref, l_ref, acc_ref)
    for p in range(np_a):
        cols = slice(p * TQ, (p + 1) * TQ)
        out_t = acc_ref[0, :, cols] / l_ref[0, :, cols] + acc_ref[1, :, cols] / l_ref[1, :, cols]
        o_ref[:, p * LANE:(p + 1) * LANE] = out_t.T.astype(BF16)


def _prompt_a(qa, qi, wi, ke, ko, vet, vot, kie, kio, bias, n, s):
    kv = pl.BlockSpec((None, s, LANE), lambda b, i: (b, 0, 0))
    vt = pl.BlockSpec((None, s // TQ, LANE, TQ), lambda b, i: (b, 0, 0, 0))
    hm = lambda h: pl.BlockSpec((h, None, TQ, LANE), lambda b, i: (0, b, i, 0))
    nblk = max(s // LANE, 2)
    return pl.pallas_call(
        functools.partial(_prompt_a_body, k_top=min(TOPK_MAX, s // 4)),
        grid=(n, s // TQ),
        in_specs=[hm(H_A // 2), hm(H_I // 2), pl.BlockSpec((None, TQ, LANE), lambda b, i: (b, i, 0)),
                  kv, kv, vt, vt, kv, kv, pl.BlockSpec(bias.shape, lambda b, i: (0,) * bias.ndim)],
        out_specs=pl.BlockSpec((None, TQ, H_A * HD_A), lambda b, i: (b, i, 0)),
        out_shape=jax.ShapeDtypeStruct((n, s, H_A * HD_A), BF16),
        scratch_shapes=[pltpu.VMEM((nblk, LANE, TQ), I32), pltpu.VMEM((nblk, LANE, TQ), F32)]
        + _sweep_scratch((H_A // 2) * TQ, 2),
        compiler_params=_cparams(2), name="prompt_a",
    )(qa, qi, wi, ke, ko, vet, vot, kie, kio, bias)


def _lambda_of(lamv, lam_init):
    return (jnp.exp(jnp.sum(lamv[0:1] * lamv[1:2], axis=1, keepdims=True))
            - jnp.exp(jnp.sum(lamv[2:3] * lamv[3:4], axis=1, keepdims=True)) + lam_init)


def _no_add(blk, nb):
    return None


def _prompt_b_body(qb_ref, k1_ref, k2_ref, vbt_ref, bias_ref, lam_ref, g_ref, o_ref,
                   m_ref, l_ref, acc_ref, *, lam_init):
    i = pl.program_id(1)
    _flash_init(m_ref, l_ref, acc_ref)
    q_all = qb_ref[...].reshape(H_B * TQ, LANE)
    var = jnp.where(i == 0, 1, 0)

    def make_near():
        return lambda g, s: s + bias_ref[var]

    _prompt_sweep(i, q_all, (k1_ref, k2_ref), (vbt_ref, vbt_ref), _no_add, make_near, m_ref, l_ref, acc_ref)
    lam = _lambda_of(lam_ref[...], lam_init)
    for h in range(H_B):
        cols = slice(h * TQ, (h + 1) * TQ)
        o = acc_ref[0, :, cols] / l_ref[0, :, cols] - lam * (acc_ref[1, :, cols] / l_ref[1, :, cols])
        y = o * lax.rsqrt(jnp.mean(o * o, axis=0, keepdims=True) + EPS) * g_ref[...] * (1.0 - lam_init)
        o_ref[:, h * LANE:(h + 1) * LANE] = y.T.astype(BF16)


def _prompt_b(qb, k1, k2, vbt, bias, lamv, g_col, lam_init, n, s):
    kv = pl.BlockSpec((None, s, LANE), lambda b, i: (b, 0, 0))
    vt = pl.BlockSpec((None, s // TQ, LANE, TQ), lambda b, i: (b, 0, 0, 0))
    full = lambda a: pl.BlockSpec(a.shape, lambda b, i: (0,) * a.ndim)
    return pl.pallas_call(
        functools.partial(_prompt_b_body, lam_init=lam_init),
        grid=(n, s // TQ),
        in_specs=[pl.BlockSpec((H_B, None, TQ, LANE), lambda b, i: (0, b, i, 0)), kv, kv, vt, full(bias), full(lamv), full(g_col)],
        out_specs=pl.BlockSpec((None, TQ, H_B * 2 * D_B), lambda b, i: (b, i, 0)),
        out_shape=jax.ShapeDtypeStruct((n, s, H_B * 2 * D_B), BF16),
        scratch_shapes=_sweep_scratch(H_B * TQ, 2),
        compiler_params=_cparams(2), name="prompt_b",
    )(qb, k1, k2, vbt, bias, lamv, g_col)


def _prompt_c_body(qc_ref, rck_ref, ckvt_ref, cm_ref, wuv_ref, o_ref, m_ref, l_ref, acc_ref):
    i = pl.program_id(1)
    _flash_init(m_ref, l_ref, acc_ref)
    q_all = qc_ref[...].reshape(H_C * TQ, 2 * LANE)
    var = jnp.where(i == 0, 1, 0)

    def make_near():
        cm = jnp.concatenate([cm_ref[var]] * H_C, axis=1)
        return lambda g, s: s + cm

    _prompt_sweep(i, q_all, (rck_ref,), (ckvt_ref,), _no_add, make_near, m_ref, l_ref, acc_ref)
    for h in range(H_C):
        cols = slice(h * TQ, (h + 1) * TQ)
        lat = (acc_ref[0, :, cols] / l_ref[0, :, cols]).T.astype(BF16)
        y = jnp.dot(lat, wuv_ref[h], preferred_element_type=F32)
        o_ref[:, h * LANE:(h + 1) * LANE] = y.astype(BF16)


def _prompt_c(qc, rck, ckvt, cmask, wuv, n, s):
    full = lambda a: pl.BlockSpec(a.shape, lambda b, i: (0,) * a.ndim)
    return pl.pallas_call(
        _prompt_c_body,
        grid=(n, s // TQ),
        in_specs=[pl.BlockSpec((H_C, None, TQ, 2 * LANE), lambda b, i: (0, b, i, 0)),
                  pl.BlockSpec((None, s, 2 * LANE), lambda b, i: (b, 0, 0)),
                  pl.BlockSpec((None, s // TQ, LANE, TQ), lambda b, i: (b, 0, 0, 0)), full(cmask), full(wuv)],
        out_specs=pl.BlockSpec((None, TQ, H_C * D_VC), lambda b, i: (b, i, 0)),
        out_shape=jax.ShapeDtypeStruct((n, s, H_C * D_VC), BF16),
        scratch_shapes=_sweep_scratch(H_C * TQ, 1),
        compiler_params=_cparams(2), name="prompt_c",
    )(qc, rck, ckvt, cmask, wuv)


def _page_specs(pps, n_pages, shape, layer):
    def spec(j):
        return pl.BlockSpec((None, None) + shape,
                            lambda b, c, pt: (layer, pt[b * n_pages + c * pps + j], 0, 0))
    return [spec(j) for j in range(pps)]


SROWS = 8


def _sample_score_body(pt_ref, *refs, pps):
    pages = refs[:pps]
    new_ref, qi_ref, w_ref, sc_ref, scnew_ref, kbuf_ref = refs[pps:]
    c = pl.program_id(1)

    def scores(kidx_t):
        s = jnp.maximum(jnp.dot(qi_ref[...], kidx_t, preferred_element_type=F32), 0.0) * w_ref[...]
        s = jnp.sum(s.reshape(DEC_SEQ, H_I, kidx_t.shape[1]), axis=1)
        return jnp.concatenate([s, jnp.full((SROWS - DEC_SEQ, kidx_t.shape[1]), -jnp.inf, F32)], axis=0)

    for j in range(pps):
        kbuf_ref[:, j * PAGE:(j + 1) * PAGE] = pages[j][2 * HD_A:CACHE_A, :].astype(BF16)
    sc = scores(kbuf_ref[...])
    for j in range(pps):
        sc_ref[j] = sc[:, j * PAGE:(j + 1) * PAGE]

    @pl.when(c == pl.num_programs(1) - 1)
    def _():
        scnew_ref[...] = scores(new_ref[2 * HD_A:CACHE_A, :].astype(BF16))


def _sample_score(pt_flat, cache_a_t, layer, new_a_t, qi_rows, w_col, n, n_pages, pps):
    per = lambda shp: pl.BlockSpec((None,) + shp, lambda b, c, pt: (b,) + (0,) * len(shp))
    grid_spec = pltpu.PrefetchScalarGridSpec(
        num_scalar_prefetch=1, grid=(n, n_pages // pps),
        in_specs=_page_specs(pps, n_pages, (CACHE_A, PAGE), layer)
        + [per((CACHE_A, PAGE)), per((DEC_SEQ * H_I, D_I)), per((DEC_SEQ * H_I, 1))],
        out_specs=[pl.BlockSpec((None, pps, SROWS, PAGE), lambda b, c, pt: (b, c, 0, 0)), per((SROWS, PAGE))],
        scratch_shapes=[pltpu.VMEM((D_I, pps * PAGE), BF16)])
    return pl.pallas_call(
        functools.partial(_sample_score_body, pps=pps), grid_spec=grid_spec, name="sample_score",
        out_shape=[jax.ShapeDtypeStruct((n, n_pages, SROWS, PAGE), F32), jax.ShapeDtypeStruct((n, SROWS, PAGE), F32)],
        compiler_params=_cparams(2),
    )(pt_flat, *([cache_a_t] * pps), new_a_t, qi_rows, w_col)


def _sample_select_body(sc_ref, scnew_ref, mask_ref, masknew_ref, key_ref, keynew_ref, *, k_top):
    g, nblk = sc_ref.shape[0], sc_ref.shape[1]
    rows = g * SROWS
    for s in range(g):
        key_ref[:, s * SROWS:(s + 1) * SROWS, :] = _score_keys(sc_ref[s])
    r = lax.broadcasted_iota(I32, (rows, PAGE), 0)
    jk = lax.broadcasted_iota(I32, (rows, PAGE), 1)
    keynew_ref[...] = _score_keys(scnew_ref[...].reshape(rows, PAGE), jk <= r % SROWS)

    def count(pred_of):
        part = jnp.sum(jnp.where(pred_of(key_ref[...]), 1.0, 0.0), axis=0) + jnp.where(pred_of(keynew_ref[...]), 1.0, 0.0)
        return jnp.sum(part, axis=1, keepdims=True)

    kf = float(k_top)
    thr = _kth_largest(lambda cand: count(lambda k: k >= cand), (rows, 1), kf)
    need = kf - count(lambda k: k > thr)
    has_tie = (thr != INT_MIN) & (count(lambda k: k >= thr) > kf)

    def plain(key):
        return jnp.where((key >= thr) & (key != INT_MIN), 0.0, NEG)

    mk_all = plain(key_ref[...])
    for s in range(g):
        mask_ref[s] = mk_all[:, s * SROWS:(s + 1) * SROWS, :]
    masknew_ref[...] = plain(keynew_ref[...]).reshape(g, SROWS, PAGE)

    @pl.when(jnp.max(jnp.where(has_tie, 1.0, 0.0)) > 0.0)
    def _():
        tri = _tri(PAGE)

        def blk_body(j, carry):
            mk, carry = _select_block(key_ref[j], thr, need, carry, tri)
            for s in range(g):
                mask_ref[s, j] = mk[s * SROWS:(s + 1) * SROWS]
            return carry

        carry = lax.fori_loop(0, nblk, blk_body, jnp.zeros((rows, 1), F32))
        mk, _ = _select_block(keynew_ref[...], thr, need, carry, tri)
        masknew_ref[...] = mk.reshape(g, SROWS, PAGE)


def _sample_select(sc, sc_new, k_top, g):
    n, nblk = sc.shape[0], sc.shape[1]
    big = pl.BlockSpec((g, nblk, SROWS, PAGE), lambda i: (i, 0, 0, 0))
    small = pl.BlockSpec((g, SROWS, PAGE), lambda i: (i, 0, 0))
    return pl.pallas_call(
        functools.partial(_sample_select_body, k_top=k_top), grid=(n // g,), name="sample_select",
        in_specs=[big, small], out_specs=[big, small],
        out_shape=[jax.ShapeDtypeStruct(sc.shape, F32), jax.ShapeDtypeStruct(sc_new.shape, F32)],
        scratch_shapes=[pltpu.VMEM((nblk, g * SROWS, PAGE), I32), pltpu.VMEM((g * SROWS, PAGE), I32)],
        compiler_params=_cparams(1),
    )(sc, sc_new)


RA_ROWS, RB_ROWS, RC_ROWS = DEC_SEQ * H_A, 2 * DEC_SEQ * H_B, DEC_SEQ * H_C


def _sample_attend_body(pt_ref, *refs, pps, lam_init):
    pa, pb, pc = refs[:pps], refs[pps:2 * pps], refs[2 * pps:3 * pps]
    (newa_ref, newb_ref, newc_ref, qa_ref, qb_ref, qc_ref, mask_ref, masknew_ref,
     ba_last_ref, ba_new_ref, bb_last_ref, bb_new_ref, cc_new_ref, lam_ref, g_ref,
     oa_ref, ob_ref, oc_ref,
     kba_ref, kbb_ref, kbc_ref, ma_ref, la_ref, acca_ref, mb_ref, lb_ref, accb_ref, mc_ref, lc_ref, accc_ref) = refs[3 * pps:]
    c = pl.program_id(1)
    n_ch = pl.num_programs(1)
    ck = pps * PAGE
    last = jnp.where(c == n_ch - 1, 1.0, 0.0)

    @pl.when(c == 0)
    def _():
        _flash_init(ma_ref, la_ref, acca_ref)
        _flash_init(mb_ref, lb_ref, accb_ref)
        _flash_init(mc_ref, lc_ref, accc_ref)
        kbc_ref[...] = jnp.zeros(kbc_ref.shape, BF16)

    for j in range(pps):
        keys = slice(j * PAGE, (j + 1) * PAGE)
        kba_ref[:, keys] = pa[j][0:LANE, :].astype(BF16)
        kbb_ref[keys, :] = pb[j][...].astype(BF16)
        kbc_ref[0:CACHE_C, keys] = pc[j][...].astype(BF16)

    def band(s, bias_ref):
        return jnp.concatenate([s[:, :ck - PAGE], s[:, ck - PAGE:] + last * bias_ref[...]], axis=1)

    def sel_rows(mask):
        return mask[0:DEC_SEQ, None, :]

    dot = lambda a, b: jnp.dot(a, b, preferred_element_type=F32)
    ka = kba_ref[...]
    sa = band(dot(qa_ref[...], ka), ba_last_ref)
    mask = jnp.concatenate([mask_ref[j] for j in range(pps)], axis=1)
    sa = (sa.reshape(DEC_SEQ, H_A, ck) + sel_rows(mask)).reshape(RA_ROWS, ck)
    _flash_update(0, sa, ka, ma_ref, la_ref, acca_ref, v_feature_major=True)
    sb = band(_nt(qb_ref[...], kbb_ref[:, 0:LANE]), bb_last_ref)
    _flash_update(0, sb, kbb_ref[:, LANE:2 * LANE], mb_ref, lb_ref, accb_ref)
    _flash_update(0, dot(qc_ref[...], kbc_ref[...]), kbc_ref[0:LANE, :], mc_ref, lc_ref, accc_ref, v_feature_major=True)

    @pl.when(c == n_ch - 1)
    def _():
        na = newa_ref[0:LANE, :].astype(BF16)
        sna = dot(qa_ref[...], na) + ba_new_ref[...]
        sna = (sna.reshape(DEC_SEQ, H_A, PAGE) + sel_rows(masknew_ref[...])).reshape(RA_ROWS, PAGE)
        _flash_update(0, sna, na, ma_ref, la_ref, acca_ref, v_feature_major=True)
        nb = newb_ref[...].astype(BF16)
        _flash_update(0, _nt(qb_ref[...], nb[:, 0:LANE]) + bb_new_ref[...], nb[:, LANE:2 * LANE], mb_ref, lb_ref, accb_ref)
        ncz = jnp.concatenate([newc_ref[...], jnp.zeros((2 * LANE - CACHE_C, PAGE), F32)], axis=0).astype(BF16)
        _flash_update(0, dot(qc_ref[...], ncz) + cc_new_ref[...], ncz[0:LANE, :], mc_ref, lc_ref, accc_ref,
                      v_feature_major=True)
        oa_ref[...] = acca_ref[0] / la_ref[0]
        ob = accb_ref[0] / lb_ref[0]
        half = RB_ROWS // 2
        o = ob[0:half] - _lambda_of(lam_ref[...], lam_init) * ob[half:RB_ROWS]
        ob_ref[...] = (_rms(o, g_ref[...]) * (1.0 - lam_init)).astype(BF16)
        oc_ref[...] = (accc_ref[0] / lc_ref[0]).astype(BF16)


def _sample_attend(pt_flat, cache_a, cache_b, cache_c, layer, new_a, new_b, new_c, qa_rows, qb_rows, qc_rows,
                   mask, mask_new, tiles, lamv, g, lam_init, n, n_pages, pps):
    n_ch = n_pages // pps
    ck = pps * PAGE
    per = lambda shp: pl.BlockSpec((None,) + shp, lambda b, c, pt: (b,) + (0,) * len(shp))
    full = lambda a: pl.BlockSpec(a.shape, lambda b, c, pt: (0,) * a.ndim)
    flash = lambda r: [pltpu.VMEM((1, r, 1), F32), pltpu.VMEM((1, r, 1), F32), pltpu.VMEM((1, r, LANE), F32)]
    grid_spec = pltpu.PrefetchScalarGridSpec(
        num_scalar_prefetch=1, grid=(n, n_ch),
        in_specs=_page_specs(pps, n_pages, (CACHE_A, PAGE), layer) + _page_specs(pps, n_pages, (PAGE, CACHE_B), layer)
        + _page_specs(pps, n_pages, (CACHE_C, PAGE), layer)
        + [per((CACHE_A, PAGE)), per((PAGE, CACHE_B)), per((CACHE_C, PAGE)),
           per((RA_ROWS, LANE)), per((RB_ROWS, LANE)), per((RC_ROWS, 2 * LANE)),
           pl.BlockSpec((None, pps, SROWS, PAGE), lambda b, c, pt: (b, c, 0, 0)), per((SROWS, PAGE))]
        + [full(t) for t in tiles] + [full(lamv), full(g)],
        out_specs=[per((RA_ROWS, LANE)), per((RB_ROWS // 2, LANE)), per((RC_ROWS, LANE))],
        scratch_shapes=[pltpu.VMEM((LANE, ck), BF16), pltpu.VMEM((ck, 2 * LANE), BF16), pltpu.VMEM((2 * LANE, ck), BF16)]
        + flash(RA_ROWS) + flash(RB_ROWS) + flash(RC_ROWS))
    return pl.pallas_call(
        functools.partial(_sample_attend_body, pps=pps, lam_init=lam_init),
        grid_spec=grid_spec, name="sample_attend",
        out_shape=[jax.ShapeDtypeStruct((n, RA_ROWS, LANE), F32), jax.ShapeDtypeStruct((n, RB_ROWS // 2, LANE), BF16),
                   jax.ShapeDtypeStruct((n, RC_ROWS, LANE), BF16)],
        compiler_params=_cparams(2),
    )(pt_flat, *([cache_a] * pps), *([cache_b] * pps), *([cache_c] * pps), new_a, new_b, new_c,
      qa_rows, qb_rows, qc_rows, mask, mask_new, *tiles, lamv, g)


def _uv_body(lat_ref, w_ref, o_ref):
    o_ref[...] = jnp.dot(lat_ref[...], w_ref[...], preferred_element_type=F32).astype(BF16)


def _uv(lat, wuv):
    h, m, _ = lat.shape
    return pl.pallas_call(
        _uv_body, grid=(h,),
        in_specs=[pl.BlockSpec((None, m, LANE), lambda i: (i, 0, 0)), pl.BlockSpec((None, LANE, LANE), lambda i: (i, 0, 0))],
        out_specs=pl.BlockSpec((None, m, LANE), lambda i: (i, 0, 0)),
        out_shape=jax.ShapeDtypeStruct((h, m, LANE), BF16), compiler_params=_cparams(1),
    )(lat, wuv)


def _merge_body(ya_ref, yb_ref, yc_ref, wa_ref, wb_ref, wc_ref, g0_ref, g1_ref, g2_ref, o_ref):
    dot = lambda a, b: jnp.dot(a[...], b[...], preferred_element_type=F32)
    o_ref[...] = (g0_ref[...].astype(F32) * dot(ya_ref, wa_ref) + g1_ref[...].astype(F32) * dot(yb_ref, wb_ref)
                  + g2_ref[...].astype(F32) * dot(yc_ref, wc_ref)).astype(BF16)


def _merge(ya, yb, yc, wa, wb, wc, gates, tm, tn):
    m = ya.shape[0]
    d = wa.shape[1]
    nt = d // tn
    xs = lambda a: pl.BlockSpec((tm, a.shape[1]), lambda j, i: (i, 0))
    ws = lambda a: pl.BlockSpec((a.shape[0], tn), lambda j, i: (0, j))
    gs = lambda k: pl.BlockSpec((tm, tn), lambda j, i: (i, k * nt + j))
    return pl.pallas_call(
        _merge_body, grid=(nt, m // tm),
        in_specs=[xs(ya), xs(yb), xs(yc), ws(wa), ws(wb), ws(wc), gs(0), gs(1), gs(2)],
        out_specs=pl.BlockSpec((tm, tn), lambda j, i: (i, j)),
        out_shape=jax.ShapeDtypeStruct((m, d), BF16), compiler_params=_cparams(2), name="merge",
    )(ya, yb, yc, wa, wb, wc, gates, gates, gates)


def _proj_resid_body(a_ref, w_ref, x_ref, gt_ref, ada_ref, o_ref):
    y = jnp.dot(a_ref[...], w_ref[...], preferred_element_type=F32)
    o_ref[...] = x_ref[...] + (gt_ref[...] + ada_ref[...]) * y


def _proj_resid(a, w, x, grp, kg, ada, tm, tn):
    m, kd = a.shape
    d = w.shape[1]
    return pl.pallas_call(
        _proj_resid_body, grid=(d // tn, m // tm),
        in_specs=[pl.BlockSpec((tm, kd), lambda j, i: (i, 0)), pl.BlockSpec((kd, tn), lambda j, i: (0, j)),
                  pl.BlockSpec((tm, tn), lambda j, i: (i, j)), grp.mod_spec(tm, tn, lambda j, i: (i, j)),
                  pl.BlockSpec((1, tn), lambda j, i: (0, j))],
        out_specs=pl.BlockSpec((tm, tn), lambda j, i: (i, j)),
        out_shape=jax.ShapeDtypeStruct((m, d), F32), compiler_params=_cparams(2), name="proj_resid",
    )(a, w, x, grp.mods[kg], ada[kg:kg + 1])


def _top_rows(x, k):
    vals = []
    for _ in range(k):
        m = jnp.max(x, axis=0, keepdims=True)
        vals.append(m)
        x = jnp.where(x == m, -jnp.inf, x)
    return jnp.concatenate(vals, axis=0)


def _peer_select_body(q_ref, k1_ref, k2_ref, s1_ref, e1_ref, s2_ref, e2_ref, thr_ref):
    thr_rows = []
    for h in range(PEER_HEADS):
        qp = q_ref[:, h * D_KEY:(h + 1) * D_KEY]
        s1 = _nt(k1_ref[h], qp)
        s2 = _nt(k2_ref[h], qp)
        a1 = _top_rows(s1, PEER_TOPK)
        a2 = _top_rows(s2, PEER_TOPK)
        cand = jnp.concatenate([a1[r:r + 1] + a2 for r in range(PEER_TOPK)], axis=0)
        best = _top_rows(cand, PEER_TOPK)
        z = jnp.sum(jnp.exp(best - best[0:1]), axis=0, keepdims=True)
        s1_ref[h] = s1
        s2_ref[h] = s2
        e1_ref[h] = jnp.exp(s1 - a1[0:1])
        e2_ref[h] = (jnp.exp(s2 - a2[0:1]) / z).astype(BF16)
        thr_rows.append(best[PEER_TOPK - 1:PEER_TOPK])
    thr_ref[...] = jnp.concatenate(thr_rows, axis=0)


def _peer_select(q, k1, k2, tm):
    m = q.shape[0]
    hk = lambda dt: (jax.ShapeDtypeStruct((PEER_HEADS, N_KEYS, m), dt),
                     pl.BlockSpec((PEER_HEADS, N_KEYS, tm), lambda i: (0, 0, i)))
    outs = [hk(F32), hk(F32), hk(F32), hk(BF16),
            (jax.ShapeDtypeStruct((PEER_HEADS, m), F32), pl.BlockSpec((PEER_HEADS, tm), lambda i: (0, i)))]
    full = lambda a: pl.BlockSpec(a.shape, lambda i: (0,) * a.ndim)
    return pl.pallas_call(
        _peer_select_body, grid=(m // tm,),
        in_specs=[pl.BlockSpec((tm, q.shape[1]), lambda i: (i, 0)), full(k1), full(k2)],
        out_specs=[o[1] for o in outs], out_shape=[o[0] for o in outs],
        compiler_params=_cparams(1), name="peer_select",
    )(q, k1, k2)


def _peer_body(h_ref, u_ref, v_ref, s1_ref, e1_ref, s2_ref, e2_ref, thr_ref, o_ref, *, eb):
    e = pl.program_id(1)

    @pl.when(e == 0)
    def _():
        o_ref[...] = jnp.zeros(o_ref.shape, F32)

    a_t = _nt(u_ref[...], h_ref[...])
    act = 0.5 * a_t * (1.0 + lax.erf(a_t * (2.0 ** -0.5)))
    rows = []
    for il in range(eb // N_KEYS):
        g = jnp.zeros((N_KEYS, a_t.shape[1]), F32)
        for h in range(PEER_HEADS):
            ssum = s1_ref[il, h:h + 1, :] + s2_ref[h]
            gate = e1_ref[il, h:h + 1, :] * e2_ref[h].astype(F32)
            g = g + jnp.where(ssum >= thr_ref[h:h + 1, :], gate, 0.0)
        rows.append(g * act[il * N_KEYS:(il + 1) * N_KEYS])
    ga = jnp.concatenate(rows, axis=0).T.astype(BF16)
    o_ref[...] += jnp.dot(ga, v_ref[...], preferred_element_type=F32)


def _peer(h, u, v, s1, e1, s2, e2, thr, tm, eb):
    m, d = h.shape
    ne = u.shape[0] // eb
    sel = lambda: pl.BlockSpec((PEER_HEADS, N_KEYS, tm), lambda i, e: (0, 0, i))
    rows = lambda: pl.BlockSpec((eb // N_KEYS, PEER_HEADS, tm), lambda i, e: (e, 0, i))
    return pl.pallas_call(
        functools.partial(_peer_body, eb=eb), grid=(m // tm, ne), name="peer",
        in_specs=[pl.BlockSpec((tm, d), lambda i, e: (i, 0)), pl.BlockSpec((eb, d), lambda i, e: (e, 0)),
                  pl.BlockSpec((eb, d), lambda i, e: (e, 0)), rows(), rows(), sel(), sel(),
                  pl.BlockSpec((PEER_HEADS, tm), lambda i, e: (0, i))],
        out_specs=pl.BlockSpec((tm, d), lambda i, e: (i, 0)),
        out_shape=jax.ShapeDtypeStruct((m, d), F32),
        compiler_params=_cparams(2),
    )(h, u, v, s1, e1, s2, e2, thr)


def _rel_bucket(dist):
    n = jnp.maximum(dist, 0)
    max_exact = N_BUCKETS // 2
    nf = jnp.maximum(n, 1).astype(F32)
    large = max_exact + (jnp.log(nf / max_exact) / math.log(MAX_DIST / max_exact) * (N_BUCKETS - max_exact)).astype(I32)
    return jnp.where(n < max_exact, n, jnp.minimum(large, N_BUCKETS - 1))


def _bias_of_dist(rel_bias, dist):
    tab = (rel_bias - rel_bias[N_BUCKETS - 1:N_BUCKETS]) * LOG2E
    return jnp.where((dist >= 0)[..., None], tab[_rel_bucket(dist)], NEG)


def _prompt_bias_tiles(rel_bias):
    t = jnp.arange(TQ)[None, :]
    s = jnp.arange(2 * TQ)[:, None]
    d_mid = t + TQ - s
    d_first = jnp.where(s < TQ, t - s, -1)
    tiles = jnp.stack([_bias_of_dist(rel_bias, d_mid), _bias_of_dist(rel_bias, d_first)])
    tiles = jnp.moveaxis(tiles, -1, 2)
    ta = tiles[:, :, :H_A].reshape(2, 2 * TQ, H_A // 2, 2, TQ)
    ta = jnp.transpose(ta, (0, 3, 1, 2, 4)).reshape(2, 2, 2 * TQ, (H_A // 2) * TQ)
    tb = tiles[:, :, H_A:].reshape(2, 2 * TQ, H_B * TQ)
    cm = jnp.stack([jnp.where(d_mid >= 0, 0.0, NEG), jnp.where(d_first >= 0, 0.0, NEG)]).astype(F32)
    return ta, tb, cm


def _sample_bias_tiles(rel_bias, past):
    t = jnp.arange(DEC_SEQ)
    s_last = past - PAGE + jnp.arange(PAGE)
    d_last = past + t[:, None] - s_last[None, :]
    j = jnp.arange(PAGE)
    d_new = jnp.where(j[None, :] < DEC_SEQ, t[:, None] - j[None, :], -1)
    b_last = _bias_of_dist(rel_bias, d_last)
    b_new = _bias_of_dist(rel_bias, d_new)
    th = lambda b, h0, h1: jnp.moveaxis(b[..., h0:h1], -1, 1).reshape(-1, PAGE)
    ba_last, ba_new = th(b_last, 0, H_A), th(b_new, 0, H_A)
    bb_last = jnp.tile(th(b_last, H_A, H_A + H_B), (2, 1))
    bb_new = jnp.tile(th(b_new, H_A, H_A + H_B), (2, 1))
    cc_new = jnp.repeat(jnp.where(d_new >= 0, 0.0, NEG).astype(F32), H_C, axis=0)
    return [ba_last, ba_new, bb_last, bb_new, cc_new]


def _rope_tables(pos):
    half = D_ROPE // 2
    inv_freq = ROPE_BASE ** (-jnp.arange(half, dtype=F32) / half)
    ang = pos.astype(F32)[:, None] * inv_freq
    cos, sin = jnp.cos(ang), jnp.sin(ang)
    pad = jnp.zeros((pos.shape[0], LANE - D_ROPE), F32)
    return jnp.concatenate([cos, cos, pad], axis=1), jnp.concatenate([-sin, sin, pad], axis=1)


def _swap_halves(w):
    h = w.shape[-1] // 2
    return jnp.concatenate([w[..., h:], w[..., :h]], axis=-1)


def _layer_weights(l, d, w_in, w_uq, w_uk, w_uv):
    splits = (H_A * HD_A, 2 * HD_A, H_I * D_I, H_I, D_I, H_B * 2 * D_B, 2 * D_B, 2 * D_B, Q_LORA, R_KV, D_ROPE, 3 * d)
    cs = np.cumsum((0,) + splits)
    qa, kva, qi, wi, ki, qb, kb, vb, cq, ckv, kr, gt = [w_in[l][:, cs[k]:cs[k + 1]] for k in range(12)]
    z = lambda n: jnp.zeros((d, n), F32)
    k_a, v_a, k1, k2 = kva[:, :HD_A], kva[:, HD_A:], kb[:, :D_B], kb[:, D_B:]
    main = jnp.concatenate([
        qa, k_a, z(64), z(64), k_a, v_a, z(64), z(64), v_a, ki, z(64), wi, z(32), ki, qi, qb,
        k1, z(64), z(64), k2, vb, cq, ckv, kr, z(LANE - D_ROPE), _swap_halves(kr), z(LANE - D_ROPE)], axis=1)
    assert main.shape[1] == N_MAIN_BLOCKS * LANE
    uq = w_uq[l].reshape(Q_LORA, H_C, D_NOPE + D_ROPE)
    rope_w = uq[:, :, D_NOPE:]
    padr = lambda w: jnp.concatenate([w, jnp.zeros((Q_LORA, H_C, LANE - D_ROPE), F32)], axis=-1).reshape(Q_LORA, H_C * LANE)
    uq_ext = jnp.concatenate([uq[:, :, :D_NOPE].reshape(Q_LORA, H_C * D_NOPE), padr(rope_w), padr(_swap_halves(rope_w))], axis=1)
    return (main.astype(BF16), gt.astype(BF16), uq_ext.astype(BF16),
            jnp.transpose(w_uk[l], (1, 2, 0)).astype(BF16), jnp.transpose(w_uv[l], (1, 0, 2)).astype(BF16))


def _tile(m, cap):
    t = min(m, cap)
    assert m % t == 0
    return t


def _project(x, grp, l, wts, g_mix, ada, g_cq, g_ckv, cos_t, sin_t, pos_tiles, tm_prep, feature_major_values):
    w_main, w_gate, uq_ext, ukT, _ = wts
    m = x.shape[0]
    h = _modulate(x, g_mix[l], grp, 1, 0, ada, _tile(m, 256))
    tm = _tile(m, 1024)
    p = _mm(h, w_main, tm, 768, F32)
    gates = _mm(h, w_gate, tm, 512, BF16, act="sigmoid")
    outs = _prep(p, g_cq[l].reshape(1, -1), g_ckv[l].reshape(1, -1), uq_ext, ukT, cos_t, sin_t, pos_tiles, tm_prep,
                 feature_major_values)
    return outs, gates


def _peer_ffn(x, grp, l, ada, g_ffn, wq, k1p, k2p, u, v):
    m, d = x.shape
    h = _modulate(x, g_ffn[l], grp, 4, 3, ada, _tile(m, 256))
    q = _mm(h, wq, _tile(m, 1024), 512, BF16)
    tm = _tile(m, 512)
    s1, e1, s2, e2, thr = _peer_select(q, k1p, k2p, _tile(m, 256))
    y = _peer(h, u, v, jnp.swapaxes(s1, 0, 1), jnp.swapaxes(e1, 0, 1), s2, e2, thr, tm, 512)
    return _resid(x, y, grp, 5, ada, _tile(m, 256))


def kernel(x_prompt, x_sample, cache_a, cache_b, cache_c, page_table, c_prompt, c_sample, w_ada, ada_emb, g_mix, w_in, g_cq, w_uq, g_ckv, w_uk, w_uv, lam_q1, lam_k1, lam_q2, lam_k2, g_subln, rel_bias, w_pa, w_pb, w_pc, w_o, g_ffn, peer_wq, peer_keys, peer_u, peer_v, g_final):
    nb, seq, d = x_prompt.shape
    ns, ts, _ = x_sample.shape
    assert ts == DEC_SEQ and seq % (2 * TQ) == 0
    depth = w_in.shape[0]
    n_pages = page_table.shape[1]
    past = n_pages * PAGE
    pps = min(32, n_pages)
    mp, ms = nb * seq, ns * ts

    c_all = jnp.concatenate([c_prompt, c_sample], axis=0)
    mod = _mm(c_all, w_ada, nb + ns, 512, F32, pre="silu").reshape(nb + ns, 6, d)
    grp_p = _Group(mp, [mod[:nb, k].reshape(nb, 1, d) for k in range(6)], seq)
    grp_s = _Group(ms, [jnp.repeat(mod[nb:, k], ts, axis=0) for k in range(6)], None)

    cos_p, sin_p = _rope_tables(jnp.arange(seq))
    cos_s, sin_s = _rope_tables(jnp.tile(past + jnp.arange(ts), ns))
    bias_a, bias_b, cmask = _prompt_bias_tiles(rel_bias)
    s_tiles = _sample_bias_tiles(rel_bias, past)
    pt_flat = page_table.reshape(-1)

    cache_a_t = jnp.swapaxes(cache_a, 2, 3)
    cache_c_t = jnp.swapaxes(cache_c, 2, 3)

    xp = x_prompt.reshape(mp, d)
    xs = x_sample.reshape(ms, d)
    rows = [[] for _ in range(6)]
    tmp_prep = _tile(seq, 256)
    tms_prep = _tile(ms, 256)
    for l in range(depth):
        lam_init = 0.8 - 0.6 * math.exp(-0.3 * l)
        lamv = jnp.stack([lam_q1[l], lam_k1[l], lam_q2[l], lam_k2[l]]).astype(F32)
        gsub = g_subln[l].reshape(1, 2 * D_B)
        ada = ada_emb[l]
        wts = _layer_weights(l, d, w_in, w_uq, w_uk, w_uv)
        wuv = wts[4]
        wpa, wpb, wpc, wo = (w.astype(BF16) for w in (w_pa[l], w_pb[l], w_pc[l], w_o[l]))

        (qa, ke, ko, kie, kio, wi, qi, qb, k1, k2, qc, rck, ra, rb, rc, vet, vot, vbt, ckvt), gates = _project(
            xp, grp_p, l, wts, g_mix, ada, g_cq, g_ckv, cos_p, sin_p, seq // tmp_prep, tmp_prep, True)
        s3 = lambda a: a.reshape(nb, seq, a.shape[-1])
        h4 = lambda a: a.reshape(a.shape[0], nb, seq, a.shape[-1])
        v4 = lambda a: a.reshape(nb, seq // TQ, LANE, TQ)
        ya = _prompt_a(h4(qa), h4(qi), s3(wi), s3(ke), s3(ko), v4(vet), v4(vot), s3(kie), s3(kio), bias_a, nb, seq)
        yb = _prompt_b(h4(qb), s3(k1), s3(k2), v4(vbt), bias_b, lamv, gsub.reshape(2 * D_B, 1), lam_init, nb, seq)
        yc = _prompt_c(h4(qc), s3(rck), v4(ckvt), cmask, wuv, nb, seq)
        tm = _tile(seq, 1024)
        merged = _merge(ya.reshape(mp, -1), yb.reshape(mp, -1), yc.reshape(mp, -1), wpa, wpb, wpc, gates, tm, 512)
        xp = _proj_resid(merged, wo, xp, grp_p, 2, ada, tm, 512)
        for k, r in zip((0, 2, 4), (ra, rb, rc)):
            rows[k].append(r.reshape(nb, seq, -1))

        (qa, ke, ko, kie, kio, wi, qi, qb, k1, k2, qc, rck, ra, rb, rc), gates = _project(
            xs, grp_s, l, wts, g_mix, ada, g_cq, g_ckv, cos_s, sin_s, ms // tms_prep, tms_prep, False)
        pad_page = lambda r: jnp.pad(r.reshape(ns, ts, -1), ((0, 0), (0, PAGE - ts), (0, 0)))
        new_a, new_b, new_c = jnp.swapaxes(pad_page(ra), 1, 2), pad_page(rb), jnp.swapaxes(pad_page(rc), 1, 2)
        heads = lambda a, w: jnp.moveaxis(a, 0, 1).reshape(ns, ts, -1, w)
        qi_rows = heads(qi, D_I).reshape(ns, ts * H_I, D_I)
        w_col = wi.reshape(ns, ts, LANE)[:, :, :H_I].reshape(ns, ts * H_I, 1)
        sc, sc_new = _sample_score(pt_flat, cache_a_t, l, new_a, qi_rows, w_col, ns, n_pages, pps)
        mask, mask_new = _sample_select(sc, sc_new, min(TOPK_MAX, (past + ts) // 4), math.gcd(ns, 8))
        qa_rows = jnp.pad(heads(qa, HD_A).reshape(ns, RA_ROWS, HD_A), ((0, 0), (0, 0), (0, LANE - HD_A)))
        qb4 = heads(qb, D_B).reshape(ns, ts, H_B, 2, D_B)
        zq = jnp.zeros((ns, ts, H_B, D_B), BF16)
        qb_rows = jnp.concatenate([jnp.concatenate([qb4[..., 0, :], zq], axis=-1).reshape(ns, ts * H_B, LANE),
                                   jnp.concatenate([zq, qb4[..., 1, :]], axis=-1).reshape(ns, ts * H_B, LANE)], axis=1)
        qc_rows = jnp.moveaxis(qc, 0, 1).reshape(ns, RC_ROWS, 2 * LANE)
        oa, ob, oc = _sample_attend(pt_flat, cache_a_t, cache_b, cache_c_t, l, new_a, new_b, new_c, qa_rows, qb_rows, qc_rows,
                                    mask, mask_new, s_tiles, lamv, gsub, lam_init, ns, n_pages, pps)
        ya = oa[:, :, HD_A:].astype(BF16).reshape(ms, H_A * HD_A)
        yb = ob.reshape(ms, H_B * 2 * D_B)
        lat = jnp.moveaxis(oc.reshape(ms, H_C, LANE), 1, 0)
        yc = jnp.moveaxis(_uv(lat, wuv), 0, 1).reshape(ms, H_C * D_VC)
        merged = _merge(ya, yb, yc, wpa, wpb, wpc, gates, ms, 512)
        xs = _proj_resid(merged, wo, xs, grp_s, 2, ada, ms, 512)
        for k, r in zip((1, 3, 5), (ra, rb, rc)):
            rows[k].append(r.reshape(ns, ts, -1))

        wq = peer_wq[l].astype(BF16)
        zk = jnp.zeros((PEER_HEADS, N_KEYS, D_KEY // 2), F32)
        k1p = jnp.concatenate([peer_keys[l, :, 0], zk], axis=-1).astype(BF16)
        k2p = jnp.concatenate([zk, peer_keys[l, :, 1]], axis=-1).astype(BF16)
        u, v = peer_u[l].astype(BF16), peer_v[l].astype(BF16)
        xp = _peer_ffn(xp, grp_p, l, ada, g_ffn, wq, k1p, k2p, u, v)
        xs = _peer_ffn(xs, grp_s, l, ada, g_ffn, wq, k1p, k2p, u, v)

    y_prompt = _final_norm(xp, g_final, _tile(mp, 256)).reshape(nb, seq, d)
    y_sample = _final_norm(xs, g_final, _tile(ms, 256)).reshape(ns, ts, d)
    return (y_prompt, y_sample) + tuple(jnp.stack(r) for r in rows)
```

```python
import functools
import math

import numpy as np
import jax
import jax.numpy as jnp
from jax import lax
from jax.experimental import pallas as pl
from jax.experimental.pallas import tpu as pltpu

F32 = jnp.float32
BF16 = jnp.bfloat16
I32 = jnp.int32

H_A, HD_A, H_I, D_I, TOPK_MAX = 24, 64, 32, 64, 256
H_B, D_B = 8, 64
H_C, D_NOPE, D_ROPE, D_VC, Q_LORA, R_KV = 12, 128, 32, 128, 768, 128
ROPE_BASE = 10000.0
N_BUCKETS, MAX_DIST = 32, 128
PEER_HEADS, N_KEYS, D_KEY, PEER_TOPK = 8, 128, 128, 16
PAGE = 128
DEC_SEQ = 4
EPS = 1e-6
CACHE_A, CACHE_B, CACHE_C = 2 * HD_A + D_I, 4 * D_B, R_KV + D_ROPE

LANE = 128
NEG = -1e30
INT_MIN = np.int32(-2 ** 31)
LOG2E = 1.4426950408889634
TQ = 128
N_MAIN_BLOCKS = 54
PAGES_PER_STEP = 32
VMEM_MB = 56


def _cparams(n_axes, vmem_mb=VMEM_MB):
    return pltpu.CompilerParams(dimension_semantics=("arbitrary",) * n_axes,
                                vmem_limit_bytes=vmem_mb * 1024 * 1024)


def _nt(a, b):
    return lax.dot_general(a, b, (((1,), (1,)), ((), ())), preferred_element_type=F32)


def _rms(x, g):
    return x * lax.rsqrt(jnp.mean(x * x, axis=-1, keepdims=True) + EPS) * g


def _mm_body(x_ref, w_ref, o_ref, *, act, pre):
    x = x_ref[...]
    if pre == "silu":
        x = (x * jax.nn.sigmoid(x)).astype(BF16)
    acc = jnp.dot(x, w_ref[...].astype(BF16), preferred_element_type=F32)
    if act == "sigmoid":
        acc = jax.nn.sigmoid(acc)
    o_ref[...] = acc.astype(o_ref.dtype)


def _mm(x, w, tm, tn, out_dtype, act=None, pre=None):
    m, kd = x.shape
    n = w.shape[1]
    return pl.pallas_call(
        functools.partial(_mm_body, act=act, pre=pre),
        grid=(n // tn, m // tm),
        in_specs=[pl.BlockSpec((tm, kd), lambda j, i: (i, 0)),
                  pl.BlockSpec((kd, tn), lambda j, i: (0, j))],
        out_specs=pl.BlockSpec((tm, tn), lambda j, i: (i, j)),
        out_shape=jax.ShapeDtypeStruct((m, n), out_dtype),
        compiler_params=_cparams(2), name="mm_%s" % (act or pre or "plain"),
    )(x, w)


class _Group:
    def __init__(self, m, mods, rows_per_mod):
        self.m = m
        self.mods = mods
        self.rows_per_mod = rows_per_mod

    def mod_spec(self, tm, tn, ij):
        if self.rows_per_mod is None:
            return pl.BlockSpec((tm, tn), lambda *g: ij(*g))
        r = self.rows_per_mod
        return pl.BlockSpec((None, 1, tn), lambda *g: ((ij(*g)[0] * tm) // r, 0, ij(*g)[1]))


def _modulate_body(x_ref, g_ref, sc_ref, sh_ref, asc_ref, ash_ref, o_ref):
    y = _rms(x_ref[...], g_ref[...])
    o_ref[...] = (y * (1.0 + sc_ref[...] + asc_ref[...]) + sh_ref[...] + ash_ref[...]).astype(BF16)


def _modulate(x, g, grp, ks, kb, ada, tm):
    m, d = x.shape
    ij = lambda i: (i, 0)
    row = pl.BlockSpec((1, d), lambda i: (0, 0))
    return pl.pallas_call(
        _modulate_body,
        grid=(m // tm,),
        in_specs=[pl.BlockSpec((tm, d), lambda i: (i, 0)), row,
                  grp.mod_spec(tm, d, ij), grp.mod_spec(tm, d, ij), row, row],
        out_specs=pl.BlockSpec((tm, d), lambda i: (i, 0)),
        out_shape=jax.ShapeDtypeStruct((m, d), BF16),
        compiler_params=_cparams(1),
    )(x, g.reshape(1, d), grp.mods[ks], grp.mods[kb], ada[ks:ks + 1], ada[kb:kb + 1])


def _resid_body(x_ref, y_ref, gt_ref, ada_ref, o_ref):
    o_ref[...] = x_ref[...] + (gt_ref[...] + ada_ref[...]) * y_ref[...]


def _resid(x, y, grp, kg, ada, tm):
    m, d = x.shape
    ij = lambda i: (i, 0)
    blk = pl.BlockSpec((tm, d), lambda i: (i, 0))
    return pl.pallas_call(
        _resid_body,
        grid=(m // tm,),
        in_specs=[blk, blk, grp.mod_spec(tm, d, ij), pl.BlockSpec((1, d), lambda i: (0, 0))],
        out_specs=blk,
        out_shape=jax.ShapeDtypeStruct((m, d), F32),
        compiler_params=_cparams(1),
    )(x, y, grp.mods[kg], ada[kg:kg + 1])


def _final_norm_body(x_ref, g_ref, o_ref):
    o_ref[...] = _rms(x_ref[...], g_ref[...])


def _final_norm(x, g, tm):
    m, d = x.shape
    blk = pl.BlockSpec((tm, d), lambda i: (i, 0))
    return pl.pallas_call(
        _final_norm_body, grid=(m // tm,),
        in_specs=[blk, pl.BlockSpec((1, d), lambda i: (0, 0))], out_specs=blk,
        out_shape=jax.ShapeDtypeStruct((m, d), F32), compiler_params=_cparams(1),
    )(x, g.reshape(1, d))


B_QA, B_KE, B_KO, B_VE, B_VO, B_KI1, B_KI2, B_QI, B_QB = 0, 12, 13, 14, 15, 16, 17, 18, 34
B_K1, B_K2, B_VB, B_CQ, B_CKV, B_KR, B_KRS = 42, 43, 44, 45, 51, 52, 53


def _prep_body(p_ref, gcq_ref, gckv_ref, wuq_ref, wuk_ref, cos_ref, sin_ref,
               qa_ref, ke_ref, ko_ref, kie_ref, kio_ref, wi_ref, qi_ref, qb_ref,
               k1_ref, k2_ref, qc_ref, rck_ref, ra_ref, rb_ref, rc_ref, *vt_refs):
    blk = lambda b, n=1: p_ref[:, b * LANE:(b + n) * LANE]
    sa = HD_A ** -0.5 * LOG2E
    for p in range(H_A // 2):
        qa_ref[p] = (blk(B_QA + p) * sa).astype(BF16)
    ke, ko, ve, vo = blk(B_KE), blk(B_KO), blk(B_VE), blk(B_VO)
    ke_ref[...] = ke.astype(BF16)
    ko_ref[...] = ko.astype(BF16)
    ki1, ki2 = blk(B_KI1), blk(B_KI2)
    lane = lax.broadcasted_iota(I32, ki2.shape, 1)
    kie_ref[...] = ki1.astype(BF16)
    kio_ref[...] = jnp.where(lane >= D_I, ki2, 0.0).astype(BF16)
    wi_ref[...] = ki2
    for p in range(H_I // 2):
        qi_ref[p] = blk(B_QI + p).astype(BF16)
    sb = D_B ** -0.5 * LOG2E
    for p in range(H_B):
        qb_ref[p] = (blk(B_QB + p) * sb).astype(BF16)
    k1, k2, vb = blk(B_K1), blk(B_K2), blk(B_VB)
    k1_ref[...] = k1.astype(BF16)
    k2_ref[...] = k2.astype(BF16)
    ra_ref[:, 0:LANE] = ke + vo
    ra_ref[:, LANE:LANE + D_I] = ki1[:, 0:D_I]
    rb_ref[:, 0:LANE] = k1 + k2
    rb_ref[:, LANE:2 * LANE] = vb
    cqn = _rms(blk(B_CQ, Q_LORA // LANE), gcq_ref[...]).astype(BF16)
    qc = jnp.dot(cqn, wuq_ref[...], preferred_element_type=F32)
    cos, sin = cos_ref[...], sin_ref[...]
    sc = (D_NOPE + D_ROPE) ** -0.5 * LOG2E
    nb = H_C
    for h in range(H_C):
        nope = qc[:, h * LANE:(h + 1) * LANE].astype(BF16)
        qlat = jnp.dot(nope, wuk_ref[h], preferred_element_type=F32)
        rp = qc[:, (nb + h) * LANE:(nb + h + 1) * LANE] * cos + qc[:, (2 * nb + h) * LANE:(2 * nb + h + 1) * LANE] * sin
        qc_ref[h, :, 0:LANE] = (qlat * sc).astype(BF16)
        qc_ref[h, :, LANE:2 * LANE] = (rp * sc).astype(BF16)
    ckvn = _rms(blk(B_CKV), gckv_ref[...])
    krope = blk(B_KR) * cos + blk(B_KRS) * sin
    rck_ref[:, 0:LANE] = ckvn.astype(BF16)
    rck_ref[:, LANE:2 * LANE] = krope.astype(BF16)
    rc_ref[:, 0:LANE] = ckvn
    rc_ref[:, LANE:LANE + D_ROPE] = krope[:, 0:D_ROPE]
    if vt_refs:
        for ref, val in zip(vt_refs, (ve, vo, vb, ckvn)):
            for j in range(val.shape[0] // LANE):
                ref[j] = val[j * LANE:(j + 1) * LANE].T.astype(BF16)


def _prep(p, gcq, gckv, wuq, wuk, cos_t, sin_t, pos_tiles, tm, feature_major_values):
    m = p.shape[0]
    tok = lambda w, dt: (jax.ShapeDtypeStruct((m, w), dt), pl.BlockSpec((tm, w), lambda i: (i, 0)))
    hm = lambda h, w: (jax.ShapeDtypeStruct((h, m, w), BF16), pl.BlockSpec((h, tm, w), lambda i: (0, i, 0)))
    vt = (jax.ShapeDtypeStruct((m // LANE, LANE, LANE), BF16), pl.BlockSpec((tm // LANE, LANE, LANE), lambda i: (i, 0, 0)))
    outs = [hm(H_A // 2, LANE)] + [tok(LANE, BF16)] * 4 + [tok(LANE, F32), hm(H_I // 2, LANE), hm(H_B, LANE)] \
        + [tok(LANE, BF16)] * 2 + [hm(H_C, 2 * LANE), tok(2 * LANE, BF16),
                                   tok(CACHE_A, F32), tok(CACHE_B, F32), tok(CACHE_C, F32)] \
        + ([vt] * 4 if feature_major_values else [])
    full = lambda a: pl.BlockSpec(a.shape, lambda i: (0,) * a.ndim)
    tab = pl.BlockSpec((tm, LANE), lambda i: (i % pos_tiles, 0))
    return pl.pallas_call(
        _prep_body, grid=(m // tm,),
        in_specs=[pl.BlockSpec((tm, p.shape[1]), lambda i: (i, 0)), full(gcq), full(gckv), full(wuq), full(wuk), tab, tab],
        out_specs=[o[1] for o in outs], out_shape=[o[0] for o in outs],
        compiler_params=_cparams(1), name="prep",
    )(p, gcq, gckv, wuq, wuk, cos_t, sin_t)


def _flash_init(m_ref, l_ref, acc_ref):
    m_ref[...] = jnp.full(m_ref.shape, NEG, F32)
    l_ref[...] = jnp.zeros(l_ref.shape, F32)
    acc_ref[...] = jnp.zeros(acc_ref.shape, F32)


def _flash_update(g, s, v, m_ref, l_ref, acc_ref, v_feature_major=False):
    m_old = m_ref[g]
    m_new = jnp.maximum(m_old, jnp.max(s, axis=1, keepdims=True))
    alpha = jnp.exp2(m_old - m_new)
    p = jnp.exp2(s - m_new)
    l_ref[g] = alpha * l_ref[g] + jnp.sum(p, axis=1, keepdims=True)
    pb = p.astype(BF16)
    pv = _nt(pb, v) if v_feature_major else jnp.dot(pb, v, preferred_element_type=F32)
    acc_ref[g] = alpha * acc_ref[g] + pv
    m_ref[g] = m_new


def _flash_update_t(g, s, vt, m_ref, l_ref, acc_ref):
    m_old = m_ref[g]
    m_new = jnp.maximum(m_old, jnp.max(s, axis=0, keepdims=True))
    alpha = jnp.exp2(m_old - m_new)
    p = jnp.exp2(s - m_new)
    l_ref[g] = alpha * l_ref[g] + jnp.sum(p, axis=0, keepdims=True)
    acc_ref[g] = alpha * acc_ref[g] + jnp.dot(vt, p.astype(BF16), preferred_element_type=F32)
    m_ref[g] = m_new


def _prompt_sweep(i, q_all, k_refs, vt_refs, make_far, make_near, m_ref, l_ref, acc_ref):
    n_far = jnp.maximum(i - 1, 0)

    def chunk(blk, nb, add):
        start = pl.multiple_of(blk * TQ, TQ)
        for g, k_ref in enumerate(k_refs):
            s = _nt(k_ref[pl.ds(start, nb * TQ), :], q_all)
            if add is not None:
                s = add(g, s)
            vt = vt_refs[g][blk] if nb == 1 else jnp.concatenate([vt_refs[g][blk], vt_refs[g][blk + 1]], axis=1)
            _flash_update_t(g, s, vt, m_ref, l_ref, acc_ref)

    def far_body(jj, carry):
        chunk(2 * jj, 2, make_far(2 * jj, 2))
        return carry

    lax.fori_loop(0, n_far // 2, far_body, 0)

    @pl.when(n_far % 2 == 1)
    def _():
        chunk(n_far - 1, 1, make_far(n_far - 1, 1))

    chunk(n_far, 2, make_near())


def _sweep_scratch(cols, halves):
    return [pltpu.VMEM((halves, 1, cols), F32), pltpu.VMEM((halves, 1, cols), F32), pltpu.VMEM((halves, LANE, cols), F32)]


def _score_keys(score, allowed=None):
    bits = lax.bitcast_convert_type(score, I32)
    key = bits ^ ((bits >> 31) & np.int32(0x7FFFFFFF))
    return key if allowed is None else jnp.where(allowed, key, INT_MIN)


def _kth_largest(count_ge, shape, k):
    def bit_body(b, t):
        cand = t ^ jnp.left_shift(np.int32(1), 31 - b)
        return jnp.where(count_ge(cand) >= k, cand, t)
    return lax.fori_loop(0, 32, bit_body, jnp.full(shape, INT_MIN, I32))


def _tri(n, lower=False):
    r = lax.broadcasted_iota(I32, (n, n), 0)
    c = lax.broadcasted_iota(I32, (n, n), 1)
    return jnp.where((r >= c) if lower else (r <= c), 1.0, 0.0).astype(BF16)


def _select_block(key, thr, need, carry, tri, keys_on_sublanes=False):
    eq = (key == thr) & (thr != INT_MIN)
    eqf = jnp.where(eq, 1.0, 0.0)
    if keys_on_sublanes:
        pref = jnp.dot(tri, eqf.astype(BF16), preferred_element_type=F32) + carry
        carry = carry + jnp.sum(eqf, axis=0, keepdims=True)
    else:
        pref = jnp.dot(eqf.astype(BF16), tri, preferred_element_type=F32) + carry
        carry = carry + jnp.sum(eqf, axis=1, keepdims=True)
    sel = (key > thr) | (eq & (pref <= need))
    return jnp.where(sel, 0.0, NEG), carry


def _prompt_a_body(qa_ref, qi_ref, wi_ref, ke_ref, ko_ref, vet_ref, vot_ref, kie_ref, kio_ref, bias_ref, o_ref,
                   key_ref, mask_ref, m_ref, l_ref, acc_ref, *, k_top):
    i = pl.program_id(1)
    n_act = i + 1
    np_i = H_I // 2
    qi_all = qi_ref[...].reshape(np_i * TQ, LANE)
    w_t = wi_ref[...].T
    krow = lax.broadcasted_iota(I32, (LANE, TQ), 0)
    qcol = lax.broadcasted_iota(I32, (LANE, TQ), 1)

    def idx_body(j, carry):
        st = pl.multiple_of(j * LANE, LANE)
        se = _nt(kie_ref[pl.ds(st, LANE), :], qi_all)
        so = _nt(kio_ref[pl.ds(st, LANE), :], qi_all)
        tot = jnp.zeros((LANE, TQ), F32)
        for p in range(np_i):
            cols = slice(p * TQ, (p + 1) * TQ)
            tot = (tot + jnp.maximum(se[:, cols], 0.0) * w_t[2 * p:2 * p + 1, :]
                   + jnp.maximum(so[:, cols], 0.0) * w_t[2 * p + 1:2 * p + 2, :])
        key_ref[j] = _score_keys(tot, (j * LANE + krow) <= (i * TQ + qcol))
        return carry

    lax.fori_loop(0, n_act, idx_body, 0)

    def count(pred_of):
        def body(j, c):
            return c + jnp.where(pred_of(key_ref[j]), 1.0, 0.0)
        return jnp.sum(lax.fori_loop(0, n_act, body, jnp.zeros((LANE, TQ), F32)), axis=0, keepdims=True)

    thr = _kth_largest(lambda cand: count(lambda k: k >= cand), (1, TQ), float(k_top))
    need = float(k_top) - count(lambda k: k > thr)
    tri = _tri(LANE, lower=True)

    def mask_body(j, carry):
        mask_ref[j], carry = _select_block(key_ref[j], thr, need, carry, tri, keys_on_sublanes=True)
        return carry

    lax.fori_loop(0, n_act, mask_body, jnp.zeros((1, TQ), F32))

    @pl.when(i == 0)
    def _():
        mask_ref[1] = jnp.full((LANE, TQ), NEG, F32)

    np_a = H_A // 2
    _flash_init(m_ref, l_ref, acc_ref)
    q_all = qa_ref[...].reshape(np_a * TQ, LANE)
    var = jnp.where(i == 0, 1, 0)

    def mask_cols(blk, nb):
        mk = mask_ref[blk] if nb == 1 else jnp.concatenate([mask_ref[blk], mask_ref[blk + 1]], axis=0)
        return jnp.concatenate([mk] * np_a, axis=1)

    def make_far(blk, nb):
        mk = mask_cols(blk, nb)
        return lambda g, s: s + mk

    def make_near():
        mk = mask_cols(jnp.maximum(i - 1, 0), 2)
        return lambda g, s: s + bias_ref[var, g] + mk

    _prompt_sweep(i, q_all, (ke_ref, ko_ref), (vet_ref, vot_ref), make_far, make_near, m_ref, l_ref, acc_ref)
    for p in range(np_a):
        cols = slice(p * TQ, (p + 1) * TQ)
        out_t = acc_ref[0, :, cols] / l_ref[0, :, cols] + acc_ref[1, :, cols] / l_ref[1, :, cols]
        o_ref[:, p * LANE:(p + 1) * LANE] = out_t.T.astype(BF16)


def _prompt_a(qa, qi, wi, ke, ko, vet, vot, kie, kio, bias, n, s):
    kv = pl.BlockSpec((None, s, LANE), lambda b, i: (b, 0, 0))
    vt = pl.BlockSpec((None, s // TQ, LANE, TQ), lambda b, i: (b, 0, 0, 0))
    hm = lambda h: pl.BlockSpec((h, None, TQ, LANE), lambda b, i: (0, b, i, 0))
    nblk = max(s // LANE, 2)
    return pl.pallas_call(
        functools.partial(_prompt_a_body, k_top=min(TOPK_MAX, s // 4)),
        grid=(n, s // TQ),
        in_specs=[hm(H_A // 2), hm(H_I // 2), pl.BlockSpec((None, TQ, LANE), lambda b, i: (b, i, 0)),
                  kv, kv, vt, vt, kv, kv, pl.BlockSpec(bias.shape, lambda b, i: (0,) * bias.ndim)],
        out_specs=pl.BlockSpec((None, TQ, H_A * HD_A), lambda b, i: (b, i, 0)),
        out_shape=jax.ShapeDtypeStruct((n, s, H_A * HD_A), BF16),
        scratch_shapes=[pltpu.VMEM((nblk, LANE, TQ), I32), pltpu.VMEM((nblk, LANE, TQ), F32)]
        + _sweep_scratch((H_A // 2) * TQ, 2),
        compiler_params=_cparams(2), name="prompt_a",
    )(qa, qi, wi, ke, ko, vet, vot, kie, kio, bias)


def _lambda_of(lamv, lam_init):
    return (jnp.exp(jnp.sum(lamv[0:1] * lamv[1:2], axis=1, keepdims=True))
            - jnp.exp(jnp.sum(lamv[2:3] * lamv[3:4], axis=1, keepdims=True)) + lam_init)


def _no_add(blk, nb):
    return None


def _prompt_b_body(qb_ref, k1_ref, k2_ref, vbt_ref, bias_ref, lam_ref, g_ref, o_ref,
                   m_ref, l_ref, acc_ref, *, lam_init):
    i = pl.program_id(1)
    _flash_init(m_ref, l_ref, acc_ref)
    q_all = qb_ref[...].reshape(H_B * TQ, LANE)
    var = jnp.where(i == 0, 1, 0)

    def make_near():
        return lambda g, s: s + bias_ref[var]

    _prompt_sweep(i, q_all, (k1_ref, k2_ref), (vbt_ref, vbt_ref), _no_add, make_near, m_ref, l_ref, acc_ref)
    lam = _lambda_of(lam_ref[...], lam_init)
    for h in range(H_B):
        cols = slice(h * TQ, (h + 1) * TQ)
        o = acc_ref[0, :, cols] / l_ref[0, :, cols] - lam * (acc_ref[1, :, cols] / l_ref[1, :, cols])
        y = o * lax.rsqrt(jnp.mean(o * o, axis=0, keepdims=True) + EPS) * g_ref[...] * (1.0 - lam_init)
        o_ref[:, h * LANE:(h + 1) * LANE] = y.T.astype(BF16)


def _prompt_b(qb, k1, k2, vbt, bias, lamv, g_col, lam_init, n, s):
    kv = pl.BlockSpec((None, s, LANE), lambda b, i: (b, 0, 0))
    vt = pl.BlockSpec((None, s // TQ, LANE, TQ), lambda b, i: (b, 0, 0, 0))
    full = lambda a: pl.BlockSpec(a.shape, lambda b, i: (0,) * a.ndim)
    return pl.pallas_call(
        functools.partial(_prompt_b_body, lam_init=lam_init),
        grid=(n, s // TQ),
        in_specs=[pl.BlockSpec((H_B, None, TQ, LANE), lambda b, i: (0, b, i, 0)), kv, kv, vt, full(bias), full(lamv), full(g_col)],
        out_specs=pl.BlockSpec((None, TQ, H_B * 2 * D_B), lambda b, i: (b, i, 0)),
        out_shape=jax.ShapeDtypeStruct((n, s, H_B * 2 * D_B), BF16),
        scratch_shapes=_sweep_scratch(H_B * TQ, 2),
        compiler_params=_cparams(2), name="prompt_b",
    )(qb, k1, k2, vbt, bias, lamv, g_col)


def _prompt_c_body(qc_ref, rck_ref, ckvt_ref, cm_ref, wuv_ref, o_ref, m_ref, l_ref, acc_ref):
    i = pl.program_id(1)
    _flash_init(m_ref, l_ref, acc_ref)
    q_all = qc_ref[...].reshape(H_C * TQ, 2 * LANE)
    var = jnp.where(i == 0, 1, 0)

    def make_near():
        cm = jnp.concatenate([cm_ref[var]] * H_C, axis=1)
        return lambda g, s: s + cm

    _prompt_sweep(i, q_all, (rck_ref,), (ckvt_ref,), _no_add, make_near, m_ref, l_ref, acc_ref)
    for h in range(H_C):
        cols = slice(h * TQ, (h + 1) * TQ)
        lat = (acc_ref[0, :, cols] / l_ref[0, :, cols]).T.astype(BF16)
        y = jnp.dot(lat, wuv_ref[h], preferred_element_type=F32)
        o_ref[:, h * LANE:(h + 1) * LANE] = y.astype(BF16)


def _prompt_c(qc, rck, ckvt, cmask, wuv, n, s):
    full = lambda a: pl.BlockSpec(a.shape, lambda b, i: (0,) * a.ndim)
    return pl.pallas_call(
        _prompt_c_body,
        grid=(n, s // TQ),
        in_specs=[pl.BlockSpec((H_C, None, TQ, 2 * LANE), lambda b, i: (0, b, i, 0)),
                  pl.BlockSpec((None, s, 2 * LANE), lambda b, i: (b, 0, 0)),
                  pl.BlockSpec((None, s // TQ, LANE, TQ), lambda b, i: (b, 0, 0, 0)), full(cmask), full(wuv)],
        out_specs=pl.BlockSpec((None, TQ, H_C * D_VC), lambda b, i: (b, i, 0)),
        out_shape=jax.ShapeDtypeStruct((n, s, H_C * D_VC), BF16),
        scratch_shapes=_sweep_scratch(H_C * TQ, 1),
        compiler_params=_cparams(2), name="prompt_c",
    )(qc, rck, ckvt, cmask, wuv)


SROWS = 8


def _paged_fetch(pt_ref, n_pages, pps, copies_of_page):
    b, c = pl.program_id(0), pl.program_id(1)
    n_ch = pl.num_programs(1)
    step = b * n_ch + c
    slot = step % 2

    def copies(slot_, b_, c_):
        base = b_ * n_pages + c_ * pps
        return [cp for j in range(pps) for cp in copies_of_page(slot_, j, pt_ref[base + j])]

    @pl.when(step == 0)
    def _():
        for cp in copies(0, 0, 0):
            cp.start()

    wrap = c + 1 == n_ch

    @pl.when(step + 1 < pl.num_programs(0) * n_ch)
    def _():
        for cp in copies(1 - slot, jnp.where(wrap, b + 1, b), jnp.where(wrap, 0, c + 1)):
            cp.start()

    for cp in copies(slot, b, c):
        cp.wait()
    return slot


def _sample_score_body(pt_ref, ca_hbm, new_ref, qi_ref, w_ref, sc_ref, scnew_ref, pg_ref, sem_ref, kbuf_ref,
                       *, pps, n_pages, layer):
    c = pl.program_id(1)
    slot = _paged_fetch(pt_ref, n_pages, pps, lambda s, j, pg: [pltpu.make_async_copy(
        ca_hbm.at[layer, pg, pl.ds(2 * HD_A, D_I)], pg_ref.at[s, j], sem_ref.at[s])])

    def scores(kidx_t):
        s = jnp.maximum(jnp.dot(qi_ref[...], kidx_t, preferred_element_type=F32), 0.0) * w_ref[...]
        s = jnp.sum(s.reshape(DEC_SEQ, H_I, kidx_t.shape[1]), axis=1)
        return jnp.concatenate([s, jnp.full((SROWS - DEC_SEQ, kidx_t.shape[1]), -jnp.inf, F32)], axis=0)

    for j in range(pps):
        kbuf_ref[:, j * PAGE:(j + 1) * PAGE] = pg_ref[slot, j].astype(BF16)
    sc = scores(kbuf_ref[...])
    for j in range(pps):
        sc_ref[j] = sc[:, j * PAGE:(j + 1) * PAGE]

    @pl.when(c == pl.num_programs(1) - 1)
    def _():
        scnew_ref[...] = scores(new_ref[2 * HD_A:CACHE_A, :].astype(BF16))


def _sample_score(pt_flat, cache_a_t, layer, new_a_t, qi_rows, w_col, n, n_pages, pps):
    per = lambda shp: pl.BlockSpec((None,) + shp, lambda b, c, pt: (b,) + (0,) * len(shp))
    grid_spec = pltpu.PrefetchScalarGridSpec(
        num_scalar_prefetch=1, grid=(n, n_pages // pps),
        in_specs=[pl.BlockSpec(memory_space=pl.ANY), per((CACHE_A, PAGE)), per((DEC_SEQ * H_I, D_I)), per((DEC_SEQ * H_I, 1))],
        out_specs=[pl.BlockSpec((None, pps, SROWS, PAGE), lambda b, c, pt: (b, c, 0, 0)), per((SROWS, PAGE))],
        scratch_shapes=[pltpu.VMEM((2, pps, D_I, PAGE), F32), pltpu.SemaphoreType.DMA((2,)),
                        pltpu.VMEM((D_I, pps * PAGE), BF16)])
    return pl.pallas_call(
        functools.partial(_sample_score_body, pps=pps, n_pages=n_pages, layer=layer),
        grid_spec=grid_spec, name="sample_score",
        out_shape=[jax.ShapeDtypeStruct((n, n_pages, SROWS, PAGE), F32), jax.ShapeDtypeStruct((n, SROWS, PAGE), F32)],
        compiler_params=_cparams(2),
    )(pt_flat, cache_a_t, new_a_t, qi_rows, w_col)


def _sample_select_body(sc_ref, scnew_ref, mask_ref, masknew_ref, key_ref, keynew_ref, *, k_top):
    g, nblk = sc_ref.shape[0], sc_ref.shape[1]
    rows = g * DEC_SEQ
    for s in range(g):
        key_ref[:, s * DEC_SEQ:(s + 1) * DEC_SEQ, :] = _score_keys(sc_ref[s, :, 0:DEC_SEQ, :])
    t = lax.broadcasted_iota(I32, (DEC_SEQ, PAGE), 0)
    jk = lax.broadcasted_iota(I32, (DEC_SEQ, PAGE), 1)
    for s in range(g):
        keynew_ref[s * DEC_SEQ:(s + 1) * DEC_SEQ, :] = _score_keys(scnew_ref[s, 0:DEC_SEQ, :], jk <= t)

    def count(pred_of):
        part = jnp.sum(jnp.where(pred_of(key_ref[...]), 1.0, 0.0), axis=0) + jnp.where(pred_of(keynew_ref[...]), 1.0, 0.0)
        return jnp.sum(part, axis=1, keepdims=True)

    kf = float(k_top)
    thr = _kth_largest(lambda cand: count(lambda k: k >= cand), (rows, 1), kf)
    need = kf - count(lambda k: k > thr)
    has_tie = (thr != INT_MIN) & (count(lambda k: k >= thr) > kf)

    def plain(key):
        return jnp.where((key >= thr) & (key != INT_MIN), 0.0, NEG)

    mask_ref[...] = jnp.zeros(mask_ref.shape, F32)
    masknew_ref[...] = jnp.zeros(masknew_ref.shape, F32)
    mk_all = plain(key_ref[...])
    mk_new = plain(keynew_ref[...])
    for s in range(g):
        mask_ref[s, :, 0:DEC_SEQ, :] = mk_all[:, s * DEC_SEQ:(s + 1) * DEC_SEQ, :]
        masknew_ref[s, 0:DEC_SEQ, :] = mk_new[s * DEC_SEQ:(s + 1) * DEC_SEQ]

    @pl.when(jnp.max(jnp.where(has_tie, 1.0, 0.0)) > 0.0)
    def _():
        tri = _tri(PAGE)

        def blk_body(j, carry):
            mk, carry = _select_block(key_ref[j], thr, need, carry, tri)
            for s in range(g):
                mask_ref[s, j, 0:DEC_SEQ, :] = mk[s * DEC_SEQ:(s + 1) * DEC_SEQ]
            return carry

        carry = lax.fori_loop(0, nblk, blk_body, jnp.zeros((rows, 1), F32))
        mk, _ = _select_block(keynew_ref[...], thr, need, carry, tri)
        for s in range(g):
            masknew_ref[s, 0:DEC_SEQ, :] = mk[s * DEC_SEQ:(s + 1) * DEC_SEQ]


def _sample_select(sc, sc_new, k_top, g):
    n, nblk = sc.shape[0], sc.shape[1]
    big = pl.BlockSpec((g, nblk, SROWS, PAGE), lambda i: (i, 0, 0, 0))
    small = pl.BlockSpec((g, SROWS, PAGE), lambda i: (i, 0, 0))
    return pl.pallas_call(
        functools.partial(_sample_select_body, k_top=k_top), grid=(n // g,), name="sample_select",
        in_specs=[big, small], out_specs=[big, small],
        out_shape=[jax.ShapeDtypeStruct(sc.shape, F32), jax.ShapeDtypeStruct(sc_new.shape, F32)],
        scratch_shapes=[pltpu.VMEM((nblk, g * DEC_SEQ, PAGE), I32), pltpu.VMEM((g * DEC_SEQ, PAGE), I32)],
        compiler_params=_cparams(1),
    )(sc, sc_new)


RA_ROWS, RB_ROWS, RC_ROWS = DEC_SEQ * H_A, 2 * DEC_SEQ * H_B, DEC_SEQ * H_C


def _sample_attend_body(pt_ref, ca_hbm, cb_hbm, cc_hbm, newa_ref, newb_ref, newc_ref, qa_ref, qb_ref, qc_ref,
                        mask_ref, masknew_ref, ba_last_ref, ba_new_ref, bb_last_ref, bb_new_ref, cc_new_ref,
                        lam_ref, g_ref, oa_ref, ob_ref, oc_ref,
                        pa_ref, pb_ref, pc_ref, sem_ref, kba_ref, kbb_ref, kbc_ref,
                        ma_ref, la_ref, acca_ref, mb_ref, lb_ref, accb_ref, mc_ref, lc_ref, accc_ref,
                        *, pps, n_pages, layer, lam_init):
    c = pl.program_id(1)
    n_ch = pl.num_programs(1)
    ck = pps * PAGE
    last = jnp.where(c == n_ch - 1, 1.0, 0.0)
    slot = _paged_fetch(pt_ref, n_pages, pps, lambda s, j, pg: [
        pltpu.make_async_copy(ca_hbm.at[layer, pg], pa_ref.at[s, j], sem_ref.at[s]),
        pltpu.make_async_copy(cb_hbm.at[layer, pg], pb_ref.at[s, j], sem_ref.at[s]),
        pltpu.make_async_copy(cc_hbm.at[layer, pg], pc_ref.at[s, j], sem_ref.at[s])])

    @pl.when(c == 0)
    def _():
        _flash_init(ma_ref, la_ref, acca_ref)
        _flash_init(mb_ref, lb_ref, accb_ref)
        _flash_init(mc_ref, lc_ref, accc_ref)
        kbc_ref[...] = jnp.zeros(kbc_ref.shape, BF16)

    for j in range(pps):
        keys = slice(j * PAGE, (j + 1) * PAGE)
        kba_ref[:, keys] = pa_ref[slot, j, 0:LANE, :].astype(BF16)
        kbb_ref[keys, :] = pb_ref[slot, j].astype(BF16)
        kbc_ref[0:CACHE_C, keys] = pc_ref[slot, j].astype(BF16)

    def band(s, bias_ref):
        return jnp.concatenate([s[:, :ck - PAGE], s[:, ck - PAGE:] + last * bias_ref[...]], axis=1)

    def sel_rows(mask):
        return mask[0:DEC_SEQ, None, :]

    dot = lambda a, b: jnp.dot(a, b, preferred_element_type=F32)
    ka = kba_ref[...]
    sa = band(dot(qa_ref[...], ka), ba_last_ref)
    mask = jnp.concatenate([mask_ref[j] for j in range(pps)], axis=1)
    sa = (sa.reshape(DEC_SEQ, H_A, ck) + sel_rows(mask)).reshape(RA_ROWS, ck)
    _flash_update(0, sa, ka, ma_ref, la_ref, acca_ref, v_feature_major=True)
    sb = band(_nt(qb_ref[...], kbb_ref[:, 0:LANE]), bb_last_ref)
    _flash_update(0, sb, kbb_ref[:, LANE:2 * LANE], mb_ref, lb_ref, accb_ref)
    _flash_update(0, dot(qc_ref[...], kbc_ref[...]), kbc_ref[0:LANE, :], mc_ref, lc_ref, accc_ref, v_feature_major=True)

    @pl.when(c == n_ch - 1)
    def _():
        na = newa_ref[0:LANE, :].astype(BF16)
        sna = dot(qa_ref[...], na) + ba_new_ref[...]
        sna = (sna.reshape(DEC_SEQ, H_A, PAGE) + sel_rows(masknew_ref[...])).reshape(RA_ROWS, PAGE)
        _flash_update(0, sna, na, ma_ref, la_ref, acca_ref, v_feature_major=True)
        nb = newb_ref[...].astype(BF16)
        _flash_update(0, _nt(qb_ref[...], nb[:, 0:LANE]) + bb_new_ref[...], nb[:, LANE:2 * LANE], mb_ref, lb_ref, accb_ref)
        ncz = jnp.concatenate([newc_ref[...], jnp.zeros((2 * LANE - CACHE_C, PAGE), F32)], axis=0).astype(BF16)
        _flash_update(0, dot(qc_ref[...], ncz) + cc_new_ref[...], ncz[0:LANE, :], mc_ref, lc_ref, accc_ref,
                      v_feature_major=True)
        oa_ref[...] = acca_ref[0] / la_ref[0]
        ob = accb_ref[0] / lb_ref[0]
        half = RB_ROWS // 2
        o = ob[0:half] - _lambda_of(lam_ref[...], lam_init) * ob[half:RB_ROWS]
        ob_ref[...] = (_rms(o, g_ref[...]) * (1.0 - lam_init)).astype(BF16)
        oc_ref[...] = (accc_ref[0] / lc_ref[0]).astype(BF16)


def _sample_attend(pt_flat, cache_a, cache_b, cache_c, layer, new_a, new_b, new_c, qa_rows, qb_rows, qc_rows,
                   mask, mask_new, tiles, lamv, g, lam_init, n, n_pages, pps):
    n_ch = n_pages // pps
    ck = pps * PAGE
    per = lambda shp: pl.BlockSpec((None,) + shp, lambda b, c, pt: (b,) + (0,) * len(shp))
    full = lambda a: pl.BlockSpec(a.shape, lambda b, c, pt: (0,) * a.ndim)
    flash = lambda r: [pltpu.VMEM((1, r, 1), F32), pltpu.VMEM((1, r, 1), F32), pltpu.VMEM((1, r, LANE), F32)]
    hbm = pl.BlockSpec(memory_space=pl.ANY)
    grid_spec = pltpu.PrefetchScalarGridSpec(
        num_scalar_prefetch=1, grid=(n, n_ch),
        in_specs=[hbm, hbm, hbm, per((CACHE_A, PAGE)), per((PAGE, CACHE_B)), per((CACHE_C, PAGE)),
                  per((RA_ROWS, LANE)), per((RB_ROWS, LANE)), per((RC_ROWS, 2 * LANE)),
                  pl.BlockSpec((None, pps, SROWS, PAGE), lambda b, c, pt: (b, c, 0, 0)), per((SROWS, PAGE))]
        + [full(t) for t in tiles] + [full(lamv), full(g)],
        out_specs=[per((RA_ROWS, LANE)), per((RB_ROWS // 2, LANE)), per((RC_ROWS, LANE))],
        scratch_shapes=[pltpu.VMEM((2, pps, CACHE_A, PAGE), F32), pltpu.VMEM((2, pps, PAGE, CACHE_B), F32),
                        pltpu.VMEM((2, pps, CACHE_C, PAGE), F32), pltpu.SemaphoreType.DMA((2,)),
                        pltpu.VMEM((LANE, ck), BF16), pltpu.VMEM((ck, 2 * LANE), BF16), pltpu.VMEM((2 * LANE, ck), BF16)]
        + flash(RA_ROWS) + flash(RB_ROWS) + flash(RC_ROWS))
    return pl.pallas_call(
        functools.partial(_sample_attend_body, pps=pps, n_pages=n_pages, layer=layer, lam_init=lam_init),
        grid_spec=grid_spec, name="sample_attend",
        out_shape=[jax.ShapeDtypeStruct((n, RA_ROWS, LANE), F32), jax.ShapeDtypeStruct((n, RB_ROWS // 2, LANE), BF16),
                   jax.ShapeDtypeStruct((n, RC_ROWS, LANE), BF16)],
        compiler_params=_cparams(2),
    )(pt_flat, cache_a, cache_b, cache_c, new_a, new_b, new_c,
      qa_rows, qb_rows, qc_rows, mask, mask_new, *tiles, lamv, g)


def _uv_body(lat_ref, w_ref, o_ref):
    o_ref[...] = jnp.dot(lat_ref[...], w_ref[...], preferred_element_type=F32).astype(BF16)


def _uv(lat, wuv):
    h, m, _ = lat.shape
    return pl.pallas_call(
        _uv_body, grid=(h,),
        in_specs=[pl.BlockSpec((None, m, LANE), lambda i: (i, 0, 0)), pl.BlockSpec((None, LANE, LANE), lambda i: (i, 0, 0))],
        out_specs=pl.BlockSpec((None, m, LANE), lambda i: (i, 0, 0)),
        out_shape=jax.ShapeDtypeStruct((h, m, LANE), BF16), compiler_params=_cparams(1),
    )(lat, wuv)


def _merge_body(ya_ref, yb_ref, yc_ref, wa_ref, wb_ref, wc_ref, g0_ref, g1_ref, g2_ref, o_ref):
    dot = lambda a, b: jnp.dot(a[...], b[...], preferred_element_type=F32)
    o_ref[...] = (g0_ref[...].astype(F32) * dot(ya_ref, wa_ref) + g1_ref[...].astype(F32) * dot(yb_ref, wb_ref)
                  + g2_ref[...].astype(F32) * dot(yc_ref, wc_ref)).astype(BF16)


def _merge(ya, yb, yc, wa, wb, wc, gates, tm, tn):
    m = ya.shape[0]
    d = wa.shape[1]
    nt = d // tn
    xs = lambda a: pl.BlockSpec((tm, a.shape[1]), lambda j, i: (i, 0))
    ws = lambda a: pl.BlockSpec((a.shape[0], tn), lambda j, i: (0, j))
    gs = lambda k: pl.BlockSpec((tm, tn), lambda j, i: (i, k * nt + j))
    return pl.pallas_call(
        _merge_body, grid=(nt, m // tm),
        in_specs=[xs(ya), xs(yb), xs(yc), ws(wa), ws(wb), ws(wc), gs(0), gs(1), gs(2)],
        out_specs=pl.BlockSpec((tm, tn), lambda j, i: (i, j)),
        out_shape=jax.ShapeDtypeStruct((m, d), BF16), compiler_params=_cparams(2), name="merge",
    )(ya, yb, yc, wa, wb, wc, gates, gates, gates)


def _proj_resid_body(a_ref, w_ref, x_ref, gt_ref, ada_ref, o_ref):
    y = jnp.dot(a_ref[...], w_ref[...], preferred_element_type=F32)
    o_ref[...] = x_ref[...] + (gt_ref[...] + ada_ref[...]) * y


def _proj_resid(a, w, x, grp, kg, ada, tm, tn):
    m, kd = a.shape
    d = w.shape[1]
    return pl.pallas_call(
        _proj_resid_body, grid=(d // tn, m // tm),
        in_specs=[pl.BlockSpec((tm, kd), lambda j, i: (i, 0)), pl.BlockSpec((kd, tn), lambda j, i: (0, j)),
                  pl.BlockSpec((tm, tn), lambda j, i: (i, j)), grp.mod_spec(tm, tn, lambda j, i: (i, j)),
                  pl.BlockSpec((1, tn), lambda j, i: (0, j))],
        out_specs=pl.BlockSpec((tm, tn), lambda j, i: (i, j)),
        out_shape=jax.ShapeDtypeStruct((m, d), F32), compiler_params=_cparams(2), name="proj_resid",
    )(a, w, x, grp.mods[kg], ada[kg:kg + 1])


def _top_rows(x, k):
    vals = []
    for _ in range(k):
        m = jnp.max(x, axis=0, keepdims=True)
        vals.append(m)
        x = jnp.where(x == m, -jnp.inf, x)
    return jnp.concatenate(vals, axis=0)


def _peer_select_body(q_ref, k1_ref, k2_ref, s1_ref, e1_ref, s2_ref, e2_ref, thr_ref):
    thr_rows = []
    for h in range(PEER_HEADS):
        qp = q_ref[:, h * D_KEY:(h + 1) * D_KEY]
        s1 = _nt(k1_ref[h], qp)
        s2 = _nt(k2_ref[h], qp)
        a1 = _top_rows(s1, PEER_TOPK)
        a2 = _top_rows(s2, PEER_TOPK)
        cand = jnp.concatenate([a1[r:r + 1] + a2 for r in range(PEER_TOPK)], axis=0)
        best = _top_rows(cand, PEER_TOPK)
        z = jnp.sum(jnp.exp(best - best[0:1]), axis=0, keepdims=True)
        s1_ref[h] = s1
        s2_ref[h] = s2
        e1_ref[h] = jnp.exp(s1 - a1[0:1])
        e2_ref[h] = (jnp.exp(s2 - a2[0:1]) / z).astype(BF16)
        thr_rows.append(best[PEER_TOPK - 1:PEER_TOPK])
    thr_ref[...] = jnp.concatenate(thr_rows, axis=0)


def _peer_select(q, k1, k2, tm):
    m = q.shape[0]
    hk = lambda dt: (jax.ShapeDtypeStruct((PEER_HEADS, N_KEYS, m), dt),
                     pl.BlockSpec((PEER_HEADS, N_KEYS, tm), lambda i: (0, 0, i)))
    outs = [hk(F32), hk(F32), hk(F32), hk(BF16),
            (jax.ShapeDtypeStruct((PEER_HEADS, m), F32), pl.BlockSpec((PEER_HEADS, tm), lambda i: (0, i)))]
    full = lambda a: pl.BlockSpec(a.shape, lambda i: (0,) * a.ndim)
    return pl.pallas_call(
        _peer_select_body, grid=(m // tm,),
        in_specs=[pl.BlockSpec((tm, q.shape[1]), lambda i: (i, 0)), full(k1), full(k2)],
        out_specs=[o[1] for o in outs], out_shape=[o[0] for o in outs],
        compiler_params=_cparams(1), name="peer_select",
    )(q, k1, k2)


def _peer_body(h_ref, u_ref, v_ref, s1_ref, e1_ref, s2_ref, e2_ref, thr_ref, o_ref, *, eb):
    e = pl.program_id(1)

    @pl.when(e == 0)
    def _():
        o_ref[...] = jnp.zeros(o_ref.shape, F32)

    a_t = _nt(u_ref[...], h_ref[...])
    act = 0.5 * a_t * (1.0 + lax.erf(a_t * (2.0 ** -0.5)))
    rows = []
    for il in range(eb // N_KEYS):
        g = jnp.zeros((N_KEYS, a_t.shape[1]), F32)
        for h in range(PEER_HEADS):
            ssum = s1_ref[il, h:h + 1, :] + s2_ref[h]
            gate = e1_ref[il, h:h + 1, :] * e2_ref[h].astype(F32)
            g = g + jnp.where(ssum >= thr_ref[h:h + 1, :], gate, 0.0)
        rows.append(g * act[il * N_KEYS:(il + 1) * N_KEYS])
    ga = jnp.concatenate(rows, axis=0).T.astype(BF16)
    o_ref[...] += jnp.dot(ga, v_ref[...], preferred_element_type=F32)


def _peer(h, u, v, s1, e1, s2, e2, thr, tm, eb):
    m, d = h.shape
    ne = u.shape[0] // eb
    sel = lambda: pl.BlockSpec((PEER_HEADS, N_KEYS, tm), lambda i, e: (0, 0, i))
    rows = lambda: pl.BlockSpec((eb // N_KEYS, PEER_HEADS, tm), lambda i, e: (e, 0, i))
    return pl.pallas_call(
        functools.partial(_peer_body, eb=eb), grid=(m // tm, ne), name="peer",
        in_specs=[pl.BlockSpec((tm, d), lambda i, e: (i, 0)), pl.BlockSpec((eb, d), lambda i, e: (e, 0)),
                  pl.BlockSpec((eb, d), lambda i, e: (e, 0)), rows(), rows(), sel(), sel(),
                  pl.BlockSpec((PEER_HEADS, tm), lambda i, e: (0, i))],
        out_specs=pl.BlockSpec((tm, d), lambda i, e: (i, 0)),
        out_shape=jax.ShapeDtypeStruct((m, d), F32),
        compiler_params=_cparams(2),
    )(h, u, v, s1, e1, s2, e2, thr)


def _rel_bucket(dist):
    n = jnp.maximum(dist, 0)
    max_exact = N_BUCKETS // 2
    nf = jnp.maximum(n, 1).astype(F32)
    large = max_exact + (jnp.log(nf / max_exact) / math.log(MAX_DIST / max_exact) * (N_BUCKETS - max_exact)).astype(I32)
    return jnp.where(n < max_exact, n, jnp.minimum(large, N_BUCKETS - 1))


def _bias_of_dist(rel_bias, dist):
    tab = (rel_bias - rel_bias[N_BUCKETS - 1:N_BUCKETS]) * LOG2E
    return jnp.where((dist >= 0)[..., None], tab[_rel_bucket(dist)], NEG)


def _prompt_bias_tiles(rel_bias):
    t = jnp.arange(TQ)[None, :]
    s = jnp.arange(2 * TQ)[:, None]
    d_mid = t + TQ - s
    d_first = jnp.where(s < TQ, t - s, -1)
    tiles = jnp.stack([_bias_of_dist(rel_bias, d_mid), _bias_of_dist(rel_bias, d_first)])
    tiles = jnp.moveaxis(tiles, -1, 2)
    ta = tiles[:, :, :H_A].reshape(2, 2 * TQ, H_A // 2, 2, TQ)
    ta = jnp.transpose(ta, (0, 3, 1, 2, 4)).reshape(2, 2, 2 * TQ, (H_A // 2) * TQ)
    tb = tiles[:, :, H_A:].reshape(2, 2 * TQ, H_B * TQ)
    cm = jnp.stack([jnp.where(d_mid >= 0, 0.0, NEG), jnp.where(d_first >= 0, 0.0, NEG)]).astype(F32)
    return ta, tb, cm


def _sample_bias_tiles(rel_bias, past):
    t = jnp.arange(DEC_SEQ)
    s_last = past - PAGE + jnp.arange(PAGE)
    d_last = past + t[:, None] - s_last[None, :]
    j = jnp.arange(PAGE)
    d_new = jnp.where(j[None, :] < DEC_SEQ, t[:, None] - j[None, :], -1)
    b_last = _bias_of_dist(rel_bias, d_last)
    b_new = _bias_of_dist(rel_bias, d_new)
    th = lambda b, h0, h1: jnp.moveaxis(b[..., h0:h1], -1, 1).reshape(-1, PAGE)
    ba_last, ba_new = th(b_last, 0, H_A), th(b_new, 0, H_A)
    bb_last = jnp.tile(th(b_last, H_A, H_A + H_B), (2, 1))
    bb_new = jnp.tile(th(b_new, H_A, H_A + H_B), (2, 1))
    cc_new = jnp.repeat(jnp.where(d_new >= 0, 0.0, NEG).astype(F32), H_C, axis=0)
    return [ba_last, ba_new, bb_last, bb_new, cc_new]


def _rope_tables(pos):
    half = D_ROPE // 2
    inv_freq = ROPE_BASE ** (-jnp.arange(half, dtype=F32) / half)
    ang = pos.astype(F32)[:, None] * inv_freq
    cos, sin = jnp.cos(ang), jnp.sin(ang)
    pad = jnp.zeros((pos.shape[0], LANE - D_ROPE), F32)
    return jnp.concatenate([cos, cos, pad], axis=1), jnp.concatenate([-sin, sin, pad], axis=1)


def _swap_halves(w):
    h = w.shape[-1] // 2
    return jnp.concatenate([w[..., h:], w[..., :h]], axis=-1)


def _layer_weights(l, d, w_in, w_uq, w_uk, w_uv):
    splits = (H_A * HD_A, 2 * HD_A, H_I * D_I, H_I, D_I, H_B * 2 * D_B, 2 * D_B, 2 * D_B, Q_LORA, R_KV, D_ROPE, 3 * d)
    cs = np.cumsum((0,) + splits)
    qa, kva, qi, wi, ki, qb, kb, vb, cq, ckv, kr, gt = [w_in[l][:, cs[k]:cs[k + 1]] for k in range(12)]
    z = lambda n: jnp.zeros((d, n), F32)
    k_a, v_a, k1, k2 = kva[:, :HD_A], kva[:, HD_A:], kb[:, :D_B], kb[:, D_B:]
    main = jnp.concatenate([
        qa, k_a, z(64), z(64), k_a, v_a, z(64), z(64), v_a, ki, z(64), wi, z(32), ki, qi, qb,
        k1, z(64), z(64), k2, vb, cq, ckv, kr, z(LANE - D_ROPE), _swap_halves(kr), z(LANE - D_ROPE)], axis=1)
    assert main.shape[1] == N_MAIN_BLOCKS * LANE
    uq = w_uq[l].reshape(Q_LORA, H_C, D_NOPE + D_ROPE)
    rope_w = uq[:, :, D_NOPE:]
    padr = lambda w: jnp.concatenate([w, jnp.zeros((Q_LORA, H_C, LANE - D_ROPE), F32)], axis=-1).reshape(Q_LORA, H_C * LANE)
    uq_ext = jnp.concatenate([uq[:, :, :D_NOPE].reshape(Q_LORA, H_C * D_NOPE), padr(rope_w), padr(_swap_halves(rope_w))], axis=1)
    return (main.astype(BF16), gt.astype(BF16), uq_ext.astype(BF16),
            jnp.transpose(w_uk[l], (1, 2, 0)).astype(BF16), jnp.transpose(w_uv[l], (1, 0, 2)).astype(BF16))


def _tile(m, cap):
    t = min(m, cap)
    assert m % t == 0
    return t


def _project(x, grp, l, wts, g_mix, ada, g_cq, g_ckv, cos_t, sin_t, pos_tiles, tm_prep, feature_major_values):
    w_main, w_gate, uq_ext, ukT, _ = wts
    m = x.shape[0]
    h = _modulate(x, g_mix[l], grp, 1, 0, ada, _tile(m, 256))
    tm = _tile(m, 1024)
    p = _mm(h, w_main, tm, 768, F32)
    gates = _mm(h, w_gate, tm, 512, BF16, act="sigmoid")
    outs = _prep(p, g_cq[l].reshape(1, -1), g_ckv[l].reshape(1, -1), uq_ext, ukT, cos_t, sin_t, pos_tiles, tm_prep,
                 feature_major_values)
    return outs, gates


def _peer_ffn(x, grp, l, ada, g_ffn, wq, k1p, k2p, u, v):
    m, d = x.shape
    h = _modulate(x, g_ffn[l], grp, 4, 3, ada, _tile(m, 256))
    q = _mm(h, wq, _tile(m, 1024), 512, BF16)
    tm = _tile(m, 512)
    s1, e1, s2, e2, thr = _peer_select(q, k1p, k2p, _tile(m, 256))
    y = _peer(h, u, v, jnp.swapaxes(s1, 0, 1), jnp.swapaxes(e1, 0, 1), s2, e2, thr, tm, 512)
    return _resid(x, y, grp, 5, ada, _tile(m, 256))


def kernel(x_prompt, x_sample, cache_a, cache_b, cache_c, page_table, c_prompt, c_sample, w_ada, ada_emb, g_mix, w_in, g_cq, w_uq, g_ckv, w_uk, w_uv, lam_q1, lam_k1, lam_q2, lam_k2, g_subln, rel_bias, w_pa, w_pb, w_pc, w_o, g_ffn, peer_wq, peer_keys, peer_u, peer_v, g_final):
    nb, seq, d = x_prompt.shape
    ns, ts, _ = x_sample.shape
    assert ts == DEC_SEQ and seq % (2 * TQ) == 0
    depth = w_in.shape[0]
    n_pages = page_table.shape[1]
    past = n_pages * PAGE
    pps = min(PAGES_PER_STEP, n_pages)
    mp, ms = nb * seq, ns * ts

    c_all = jnp.concatenate([c_prompt, c_sample], axis=0)
    mod = _mm(c_all, w_ada, nb + ns, 512, F32, pre="silu").reshape(nb + ns, 6, d)
    grp_p = _Group(mp, [mod[:nb, k].reshape(nb, 1, d) for k in range(6)], seq)
    grp_s = _Group(ms, [jnp.repeat(mod[nb:, k], ts, axis=0) for k in range(6)], None)

    cos_p, sin_p = _rope_tables(jnp.arange(seq))
    cos_s, sin_s = _rope_tables(jnp.tile(past + jnp.arange(ts), ns))
    bias_a, bias_b, cmask = _prompt_bias_tiles(rel_bias)
    s_tiles = _sample_bias_tiles(rel_bias, past)
    pt_flat = page_table.reshape(-1)

    cache_a_t = jnp.swapaxes(cache_a, 2, 3)
    cache_c_t = jnp.swapaxes(cache_c, 2, 3)

    xp = x_prompt.reshape(mp, d)
    xs = x_sample.reshape(ms, d)
    rows = [[] for _ in range(6)]
    tmp_prep = _tile(seq, 256)
    tms_prep = _tile(ms, 256)
    for l in range(depth):
        lam_init = 0.8 - 0.6 * math.exp(-0.3 * l)
        lamv = jnp.stack([lam_q1[l], lam_k1[l], lam_q2[l], lam_k2[l]]).astype(F32)
        gsub = g_subln[l].reshape(1, 2 * D_B)
        ada = ada_emb[l]
        wts = _layer_weights(l, d, w_in, w_uq, w_uk, w_uv)
        wuv = wts[4]
        wpa, wpb, wpc, wo = (w.astype(BF16) for w in (w_pa[l], w_pb[l], w_pc[l], w_o[l]))

        (qa, ke, ko, kie, kio, wi, qi, qb, k1, k2, qc, rck, ra, rb, rc, vet, vot, vbt, ckvt), gates = _project(
            xp, grp_p, l, wts, g_mix, ada, g_cq, g_ckv, cos_p, sin_p, seq // tmp_prep, tmp_prep, True)
        s3 = lambda a: a.reshape(nb, seq, a.shape[-1])
        h4 = lambda a: a.reshape(a.shape[0], nb, seq, a.shape[-1])
        v4 = lambda a: a.reshape(nb, seq // TQ, LANE, TQ)
        ya = _prompt_a(h4(qa), h4(qi), s3(wi), s3(ke), s3(ko), v4(vet), v4(vot), s3(kie), s3(kio), bias_a, nb, seq)
        yb = _prompt_b(h4(qb), s3(k1), s3(k2), v4(vbt), bias_b, lamv, gsub.reshape(2 * D_B, 1), lam_init, nb, seq)
        yc = _prompt_c(h4(qc), s3(rck), v4(ckvt), cmask, wuv, nb, seq)
        tm = _tile(seq, 1024)
        merged = _merge(ya.reshape(mp, -1), yb.reshape(mp, -1), yc.reshape(mp, -1), wpa, wpb, wpc, gates, tm, 512)
        xp = _proj_resid(merged, wo, xp, grp_p, 2, ada, tm, 512)
        for k, r in zip((0, 2, 4), (ra, rb, rc)):
            rows[k].append(r.reshape(nb, seq, -1))

        (qa, ke, ko, kie, kio, wi, qi, qb, k1, k2, qc, rck, ra, rb, rc), gates = _project(
            xs, grp_s, l, wts, g_mix, ada, g_cq, g_ckv, cos_s, sin_s, ms // tms_prep, tms_prep, False)
        pad_page = lambda r: jnp.pad(r.reshape(ns, ts, -1), ((0, 0), (0, PAGE - ts), (0, 0)))
        new_a, new_b, new_c = jnp.swapaxes(pad_page(ra), 1, 2), pad_page(rb), jnp.swapaxes(pad_page(rc), 1, 2)
        heads = lambda a, w: jnp.moveaxis(a, 0, 1).reshape(ns, ts, -1, w)
        qi_rows = heads(qi, D_I).reshape(ns, ts * H_I, D_I)
        w_col = wi.reshape(ns, ts, LANE)[:, :, :H_I].reshape(ns, ts * H_I, 1)
        sc, sc_new = _sample_score(pt_flat, cache_a_t, l, new_a, qi_rows, w_col, ns, n_pages, pps)
        mask, mask_new = _sample_select(sc, sc_new, min(TOPK_MAX, (past + ts) // 4), math.gcd(ns, 8))
        qa_rows = jnp.pad(heads(qa, HD_A).reshape(ns, RA_ROWS, HD_A), ((0, 0), (0, 0), (0, LANE - HD_A)))
        qb4 = heads(qb, D_B).reshape(ns, ts, H_B, 2, D_B)
        zq = jnp.zeros((ns, ts, H_B, D_B), BF16)
        qb_rows = jnp.concatenate([jnp.concatenate([qb4[..., 0, :], zq], axis=-1).reshape(ns, ts * H_B, LANE),
                                   jnp.concatenate([zq, qb4[..., 1, :]], axis=-1).reshape(ns, ts * H_B, LANE)], axis=1)
        qc_rows = jnp.moveaxis(qc, 0, 1).reshape(ns, RC_ROWS, 2 * LANE)
        oa, ob, oc = _sample_attend(pt_flat, cache_a_t, cache_b, cache_c_t, l, new_a, new_b, new_c, qa_rows, qb_rows, qc_rows,
                                    mask, mask_new, s_tiles, lamv, gsub, lam_init, ns, n_pages, pps)
        ya = oa[:, :, HD_A:].astype(BF16).reshape(ms, H_A * HD_A)
        yb = ob.reshape(ms, H_B * 2 * D_B)
        lat = jnp.moveaxis(oc.reshape(ms, H_C, LANE), 1, 0)
        yc = jnp.moveaxis(_uv(lat, wuv), 0, 1).reshape(ms, H_C * D_VC)
        merged = _merge(ya, yb, yc, wpa, wpb, wpc, gates, ms, 512)
        xs = _proj_resid(merged, wo, xs, grp_s, 2, ada, ms, 512)
        for k, r in zip((1, 3, 5), (ra, rb, rc)):
            rows[k].append(r.reshape(ns, ts, -1))

        wq = peer_wq[l].astype(BF16)
        zk = jnp.zeros((PEER_HEADS, N_KEYS, D_KEY // 2), F32)
        k1p = jnp.concatenate([peer_keys[l, :, 0], zk], axis=-1).astype(BF16)
        k2p = jnp.concatenate([zk, peer_keys[l, :, 1]], axis=-1).astype(BF16)
        u, v = peer_u[l].astype(BF16), peer_v[l].astype(BF16)
        xp = _peer_ffn(xp, grp_p, l, ada, g_ffn, wq, k1p, k2p, u, v)
        xs = _peer_ffn(xs, grp_s, l, ada, g_ffn, wq, k1p, k2p, u, v)

    y_prompt = _final_norm(xp, g_final, _tile(mp, 256)).reshape(nb, seq, d)
    y_sample = _final_norm(xs, g_final, _tile(ms, 256)).reshape(ns, ts, d)
    return (y_prompt, y_sample) + tuple(jnp.stack(r) for r in rows)
```
